```python
import math
import jax, jax.numpy as jnp
from jax import lax
import numpy as np

D_MODEL = 2048
BATCH = 8
SEQ = 2048
DEPTH = 2
DEC_BATCH = 2
DEC_SEQ = 16384
PAST_LEN = 128

D_CONV = 3 * D_MODEL // 8
D_ATT = 3 * D_MODEL // 8
D_MEM = D_MODEL - D_CONV - D_ATT
ATT_HEAD_DIM = 64
N_ATT_HEADS = D_ATT // ATT_HEAD_DIM
N_MEM_HEADS = 4
MEM_HEAD_DIM = D_MEM // N_MEM_HEADS
N_MEM_TOKENS = 256
D_IN = 2 * D_CONV + 3 * D_ATT + D_MEM
CONV_WIDTH = 31
CONV_PAD = CONV_WIDTH // 2
DILATED_CONFIGS = ((128, 1), (512, 4), (2048, 16))
ROT_DIM = ATT_HEAD_DIM // 4
ROPE_THETA = 500000.0
D_FF = 5632
N_EXPERTS = 8
TOP_K = 2
EXPERT_FF = 7 * D_MODEL // 2
N_DENSE = (DEPTH + 1) // 2
N_MOE = DEPTH // 2
ALPHA = (2 * DEPTH) ** 0.25
BETA = (8 * DEPTH) ** -0.25
LN_EPS = 1e-5
NEG_INF = -1e30

kernel_name = 'hybrid_conv_dilated_memory_encoder'


def _layernorm(x, g, b):
    xf = x.astype(jnp.float32)
    mu = jnp.mean(xf, axis=-1, keepdims=True)
    var = jnp.mean(jnp.square(xf - mu), axis=-1, keepdims=True)
    y = (xf - mu) * lax.rsqrt(var + LN_EPS)
    return (y * g.astype(jnp.float32) + b.astype(jnp.float32)).astype(x.dtype)


def _partial_rotary(t, positions):
    half = ROT_DIM // 2
    inv_freq = jnp.power(ROPE_THETA, -jnp.arange(0, ROT_DIM, 2, dtype=jnp.float32) / ROT_DIM)
    ang = positions[:, None] * inv_freq[None, :]
    cos = jnp.cos(ang)[None, :, None, :]
    sin = jnp.sin(ang)[None, :, None, :]
    tf = t.astype(jnp.float32)
    t1 = tf[..., :half]
    t2 = tf[..., half:ROT_DIM]
    out = jnp.concatenate([t1 * cos - t2 * sin, t2 * cos + t1 * sin, tf[..., ROT_DIM:]], axis=-1)
    return out.astype(t.dtype)


def _conformer_conv(a, g, conv_w, conv_b, ln_g, ln_b):
    u = a * jax.nn.sigmoid(g)
    y = lax.conv_general_dilated(
        u, conv_w[:, None, :], window_strides=(1,), padding=((CONV_PAD, CONV_PAD),),
        dimension_numbers=('NWC', 'WIO', 'NWC'), feature_group_count=u.shape[-1])
    y = _layernorm(y + conv_b, ln_g, ln_b)
    return jax.nn.silu(y)


def _banded_attention(q, k, v, n_side):
    N, L, H, Dh = q.shape
    blk = n_side
    nb = -(-L // blk)
    Lp = nb * blk
    tail = Lp - L
    qb = jnp.pad(q, ((0, 0), (0, tail), (0, 0), (0, 0))).reshape(N, nb, blk, H, Dh)
    kb = jnp.pad(k, ((0, 0), (blk, blk + tail), (0, 0), (0, 0))).reshape(N, nb + 2, blk, H, Dh)
    vb = jnp.pad(v, ((0, 0), (blk, blk + tail), (0, 0), (0, 0))).reshape(N, nb + 2, blk, H, Dh)
    kw = jnp.concatenate([kb[:, :-2], kb[:, 1:-1], kb[:, 2:]], axis=2)
    vw = jnp.concatenate([vb[:, :-2], vb[:, 1:-1], vb[:, 2:]], axis=2)
    s = jnp.einsum('nbqhd,nbkhd->nbhqk', qb, kw, preferred_element_type=jnp.float32)
    s = s * (Dh ** -0.5)
    blocks = jnp.arange(nb)
    qpos = blocks[:, None] * blk + jnp.arange(blk)[None, :]
    kpos = (blocks[:, None] - 1) * blk + jnp.arange(3 * blk)[None, :]
    rel = kpos[:, None, :] - qpos[:, :, None]
    valid = (jnp.abs(rel) <= n_side) & (kpos[:, None, :] >= 0) & (kpos[:, None, :] < L)
    s = jnp.where(valid[None, :, None, :, :], s, NEG_INF)
    m = jnp.max(s, axis=-1, keepdims=True)
    p = jnp.exp(s - m)
    den = jnp.sum(p, axis=-1)
    num = jnp.einsum('nbhqk,nbkhd->nbqhd', p, vw.astype(jnp.float32))
    num = num.reshape(N, Lp, H, Dh)[:, :L]
    den = den.transpose(0, 1, 3, 2).reshape(N, Lp, H)[:, :L]
    m = m[..., 0].transpose(0, 1, 3, 2).reshape(N, Lp, H)[:, :L]
    return num, den, m


def _to_residue_major(t, dil):
    B, S = t.shape[:2]
    rest = t.shape[2:]
    t = t.reshape((B, S // dil, dil) + rest)
    t = jnp.moveaxis(t, 2, 1)
    return t.reshape((B * dil, S // dil) + rest)


def _from_residue_major(t, B, dil):
    L = t.shape[1]
    rest = t.shape[2:]
    t = t.reshape((B, dil, L) + rest)
    t = jnp.moveaxis(t, 1, 2)
    return t.reshape((B, L * dil) + rest)


def _dilated_attention(q, k, v):
    B = q.shape[0]
    nums, dens, maxs = [], [], []
    for window, dil in DILATED_CONFIGS:
        n_side = window // (2 * dil)
        num, den, m = _banded_attention(_to_residue_major(q, dil), _to_residue_major(k, dil),
                                        _to_residue_major(v, dil), n_side)
        nums.append(_from_residue_major(num, B, dil))
        dens.append(_from_residue_major(den, B, dil))
        maxs.append(_from_residue_major(m, B, dil))
    m_all = jnp.maximum(jnp.maximum(maxs[0], maxs[1]), maxs[2])
    scales = [jnp.exp(mg - m_all) for mg in maxs]
    numer = sum(n * w[..., None] for n, w in zip(nums, scales))
    denom = sum(d * w for d, w in zip(dens, scales))
    return (numer / denom[..., None]).astype(q.dtype)


def _memory_attention(qm, mem, w_mem_kv):
    B, S, _ = qm.shape
    M = mem.shape[1]
    kv = mem @ w_mem_kv
    km = kv[..., :D_MEM].reshape(B, M, N_MEM_HEADS, MEM_HEAD_DIM)
    vm = kv[..., D_MEM:].reshape(B, M, N_MEM_HEADS, MEM_HEAD_DIM)
    q = qm.reshape(B, S, N_MEM_HEADS, MEM_HEAD_DIM)
    s = jnp.einsum('bshd,bmhd->bhsm', q, km, preferred_element_type=jnp.float32) * (MEM_HEAD_DIM ** -0.5)
    p = jax.nn.softmax(s, axis=-1)
    o = jnp.einsum('bhsm,bmhd->bshd', p, vm.astype(jnp.float32))
    return o.reshape(B, S, D_MEM).astype(qm.dtype)


def _mixer(x, mem, w_in, conv_w, conv_b, conv_ln_g, conv_ln_b, w_mem_kv, w_out):
    B, S, _ = x.shape
    h = x @ w_in
    cuts = np.cumsum([D_CONV, D_CONV, D_ATT, D_ATT, D_ATT]).tolist()
    a, g, q, k, v, qm = jnp.split(h, cuts, axis=-1)
    conv_out = _conformer_conv(a, g, conv_w, conv_b, conv_ln_g, conv_ln_b)
    positions = jnp.arange(S, dtype=jnp.float32)
    q = _partial_rotary(q.reshape(B, S, N_ATT_HEADS, ATT_HEAD_DIM), positions)
    k = _partial_rotary(k.reshape(B, S, N_ATT_HEADS, ATT_HEAD_DIM), positions)
    v = v.reshape(B, S, N_ATT_HEADS, ATT_HEAD_DIM)
    att_out = _dilated_attention(q, k, v).reshape(B, S, D_ATT)
    mem_out = _memory_attention(qm, mem, w_mem_kv)
    return jnp.concatenate([conv_out, att_out, mem_out], axis=-1) @ w_out


def _swiglu(x, w1, w3, w2):
    return (jax.nn.silu(x @ w1) * (x @ w3)) @ w2


def _moe_swiglu(x, router, w1, w3, w2):
    B, S, D = x.shape
    t = x.reshape(B * S, D)
    logits = (t @ router).astype(jnp.float32)
    top_vals, top_idx = lax.top_k(logits, TOP_K)
    gates = jax.nn.softmax(top_vals, axis=-1)
    combine = jnp.sum(jax.nn.one_hot(top_idx, N_EXPERTS, dtype=jnp.float32) * gates[..., None], axis=1)
    combine = combine.astype(t.dtype)
    out = jnp.zeros_like(t)
    for e in range(N_EXPERTS):
        out = out + combine[:, e:e + 1] * _swiglu(t, w1[e], w3[e], w2[e])
    return out.reshape(B, S, D)


def _trunk(x, mem, w_in, conv_w, conv_b, conv_ln_g, conv_ln_b, w_mem_kv, w_out, ln1_g, ln1_b,
           ffn_w1, ffn_w3, ffn_w2, moe_router, moe_w1, moe_w3, moe_w2, ln2_g, ln2_b):
    for l in range(DEPTH):
        y = _mixer(x, mem, w_in[l], conv_w[l], conv_b[l], conv_ln_g[l], conv_ln_b[l], w_mem_kv[l], w_out[l])
        x = _layernorm(ALPHA * x + y, ln1_g[l], ln1_b[l])
        if l % 2 == 0:
            i = l // 2
            f = _swiglu(x, ffn_w1[i], ffn_w3[i], ffn_w2[i])
        else:
            i = l // 2
            f = _moe_swiglu(x, moe_router[i], moe_w1[i], moe_w3[i], moe_w2[i])
        x = _layernorm(ALPHA * x + f, ln2_g[l], ln2_b[l])
    return x


def setup_inputs(seed: int = 0) -> dict:
    key = jax.random.key(seed)
    ks = jax.random.split(key, 24)
    f32 = jnp.float32
    nrm = lambda k, shape, scale: jax.random.normal(k, shape, f32) * scale
    return {
        'x_prompt': nrm(ks[0], (BATCH, SEQ, D_MODEL), 1.0),
        'x_sample': nrm(ks[1], (DEC_BATCH, DEC_SEQ, D_MODEL), 1.0),
        'mem_prompt': nrm(ks[2], (BATCH, N_MEM_TOKENS, D_MODEL), 1.0),
        'mem_sample': nrm(ks[3], (DEC_BATCH, N_MEM_TOKENS, D_MODEL), 1.0),
        'w_in': nrm(ks[4], (DEPTH, D_MODEL, D_IN), D_MODEL ** -0.5),
        'conv_w': nrm(ks[5], (DEPTH, CONV_WIDTH, D_CONV), CONV_WIDTH ** -0.5),
        'conv_b': nrm(ks[6], (DEPTH, D_CONV), 0.02),
        'conv_ln_g': 1.0 + nrm(ks[7], (DEPTH, D_CONV), 0.02),
        'conv_ln_b': nrm(ks[8], (DEPTH, D_CONV), 0.02),
        'w_mem_kv': nrm(ks[9], (DEPTH, D_MODEL, 2 * D_MEM), D_MODEL ** -0.5),
        'w_out': nrm(ks[10], (DEPTH, D_MODEL, D_MODEL), BETA * D_MODEL ** -0.5),
        'ln1_g': 1.0 + nrm(ks[11], (DEPTH, D_MODEL), 0.02),
        'ln1_b': nrm(ks[12], (DEPTH, D_MODEL), 0.02),
        'ffn_w1': nrm(ks[13], (N_DENSE, D_MODEL, D_FF), D_MODEL ** -0.5),
        'ffn_w3': nrm(ks[14], (N_DENSE, D_MODEL, D_FF), D_MODEL ** -0.5),
        'ffn_w2': nrm(ks[15], (N_DENSE, D_FF, D_MODEL), BETA * D_FF ** -0.5),
        'moe_router': nrm(ks[16], (N_MOE, D_MODEL, N_EXPERTS), D_MODEL ** -0.5),
        'moe_w1': nrm(ks[17], (N_MOE, N_EXPERTS, D_MODEL, EXPERT_FF), D_MODEL ** -0.5),
        'moe_w3': nrm(ks[18], (N_MOE, N_EXPERTS, D_MODEL, EXPERT_FF), D_MODEL ** -0.5),
        'moe_w2': nrm(ks[19], (N_MOE, N_EXPERTS, EXPERT_FF, D_MODEL), BETA * EXPERT_FF ** -0.5),
        'ln2_g': 1.0 + nrm(ks[20], (DEPTH, D_MODEL), 0.02),
        'ln2_b': nrm(ks[21], (DEPTH, D_MODEL), 0.02),
    }


def reference(x_prompt, x_sample, mem_prompt, mem_sample, w_in, conv_w, conv_b, conv_ln_g, conv_ln_b,
              w_mem_kv, w_out, ln1_g, ln1_b, ffn_w1, ffn_w3, ffn_w2, moe_router, moe_w1, moe_w3, moe_w2,
              ln2_g, ln2_b):
    y_prompt = _trunk(x_prompt, mem_prompt, w_in, conv_w, conv_b, conv_ln_g, conv_ln_b, w_mem_kv, w_out,
                      ln1_g, ln1_b, ffn_w1, ffn_w3, ffn_w2, moe_router, moe_w1, moe_w3, moe_w2, ln2_g, ln2_b)
    y_sample = _trunk(x_sample, mem_sample, w_in, conv_w, conv_b, conv_ln_g, conv_ln_b, w_mem_kv, w_out,
                      ln1_g, ln1_b, ffn_w1, ffn_w3, ffn_w2, moe_router, moe_w1, moe_w3, moe_w2, ln2_g, ln2_b)
    return (y_prompt, y_sample)
```

```python
import functools
import math

import jax
import jax.numpy as jnp
import numpy as np
from jax import lax
from jax.experimental import pallas as pl
from jax.experimental.pallas import tpu as pltpu

F32 = jnp.float32
BF16 = jnp.bfloat16

D_MODEL = 2048
DEPTH = 2
D_CONV = 768
D_ATT = 768
D_MEM = 512
ATT_HEAD_DIM = 64
N_MEM_HEADS = 4
MEM_HEAD_DIM = 128
N_MEM_TOKENS = 256
D_IN = 2 * D_CONV + 3 * D_ATT + D_MEM
CONV_WIDTH = 31
CONV_PAD = CONV_WIDTH // 2
DILATIONS = (1, 4, 16)
N_SIDE = 64
ROT_DIM = 16
ROPE_THETA = 500000.0
N_EXPERTS = 8
ALPHA = (2 * DEPTH) ** 0.25
LN_EPS = 1e-5
NEG_INF = -1e30

LANES = 128
SUBLANES = 8
COL_Q = (2 * D_CONV) // LANES
COL_K = COL_Q + D_ATT // LANES
COL_V = COL_K + D_ATT // LANES
COL_QM = COL_V + D_ATT // LANES
N_PAIRS = D_ATT // LANES

VMEM_LIMIT = 56 * 1024 * 1024


def _cp(*sem):
    return pltpu.CompilerParams(dimension_semantics=sem, vmem_limit_bytes=VMEM_LIMIT)


def _layernorm_rows(z, g, b):
    mu = jnp.mean(z, axis=-1, keepdims=True)
    zc = z - mu
    var = jnp.mean(zc * zc, axis=-1, keepdims=True)
    return zc * lax.rsqrt(var + LN_EPS) * g + b


def _matmul_kernel(x_ref, w_ref, o_ref):
    o_ref[...] = jnp.dot(x_ref[...].astype(BF16), w_ref[...],
                         preferred_element_type=F32).astype(o_ref.dtype)


def _matmul(x, w, tm, tn, out_dtype):
    M, K = x.shape
    N = w.shape[1]
    return pl.pallas_call(
        _matmul_kernel,
        grid=(N // tn, M // tm),
        in_specs=[pl.BlockSpec((tm, K), lambda j, i: (i, 0)),
                  pl.BlockSpec((K, tn), lambda j, i: (0, j))],
        out_specs=pl.BlockSpec((tm, tn), lambda j, i: (i, j)),
        out_shape=jax.ShapeDtypeStruct((M, N), out_dtype),
        compiler_params=_cp("parallel", "parallel"),
        name="proj_matmul",
    )(x, w)


CONV_HALO = 16
CONV_ROWS = 16


def _conv_kernel(ap_ref, am_ref, an_ref, gp_ref, gm_ref, gn_ref, w_ref, cb_ref, lg_ref, lb_ref,
                 o_ref, u_ref, ush_ref, *, ts):
    i = pl.program_id(1)
    last = pl.num_programs(1) - 1

    def glu(a_ref, g_ref):
        return a_ref[...].astype(F32) * jax.nn.sigmoid(g_ref[...].astype(F32))

    u_ref[pl.ds(CONV_HALO, ts), :] = glu(am_ref, gm_ref)
    u_ref[pl.ds(0, CONV_HALO), :] = jnp.where(i > 0, glu(ap_ref, gp_ref), 0.0)
    u_ref[pl.ds(CONV_HALO + ts, CONV_HALO), :] = jnp.where(i < last, glu(an_ref, gn_ref), 0.0)

    cb = cb_ref[...]
    lg = lg_ref[...]
    lb = lb_ref[...]

    def fill(c, carry):
        r = pl.multiple_of(c * SUBLANES, SUBLANES)
        two = u_ref[pl.ds(r, 2 * SUBLANES), :]
        for m in range(SUBLANES):
            ush_ref[m, pl.ds(r, SUBLANES), :] = two[m:m + SUBLANES]
        return carry

    lax.fori_loop(0, (ts + 2 * CONV_HALO) // SUBLANES - 1, fill, 0)

    def body(c, carry):
        r0 = pl.multiple_of(c * CONV_ROWS, CONV_ROWS)
        acc = jnp.zeros((CONV_ROWS, D_CONV), F32)
        for t in range(CONV_WIDTH):
            q, m = divmod(t + CONV_HALO - CONV_PAD, SUBLANES)
            acc = acc + w_ref[t] * ush_ref[m, pl.ds(r0 + q * SUBLANES, CONV_ROWS), :]
        y = _layernorm_rows(acc + cb, lg, lb)
        o_ref[pl.ds(r0, CONV_ROWS), :] = (y * jax.nn.sigmoid(y)).astype(o_ref.dtype)
        return carry

    lax.fori_loop(0, ts // CONV_ROWS, body, 0)


def _conformer_conv(h, conv_w, conv_b, ln_g, ln_b, ts):
    B, S, _ = h.shape
    nh = ts // CONV_HALO
    n_halo = S // CONV_HALO
    w_b = jnp.broadcast_to(conv_w[:, None, :], (CONV_WIDTH, CONV_ROWS, D_CONV)).astype(F32)

    def main(col):
        return pl.BlockSpec((None, ts, D_CONV), lambda b, i: (b, i, col))

    def prev(col):
        return pl.BlockSpec((None, CONV_HALO, D_CONV),
                            lambda b, i: (b, jnp.maximum(i * nh - 1, 0), col))

    def nxt(col):
        return pl.BlockSpec((None, CONV_HALO, D_CONV),
                            lambda b, i: (b, jnp.minimum((i + 1) * nh, n_halo - 1), col))

    vec = pl.BlockSpec((1, D_CONV), lambda b, i: (0, 0))
    return pl.pallas_call(
        functools.partial(_conv_kernel, ts=ts),
        grid=(B, S // ts),
        in_specs=[prev(0), main(0), nxt(0), prev(1), main(1), nxt(1),
                  pl.BlockSpec((CONV_WIDTH, CONV_ROWS, D_CONV), lambda b, i: (0, 0, 0)),
                  vec, vec, vec],
        out_specs=pl.BlockSpec((None, ts, D_CONV), lambda b, i: (b, i, 0)),
        out_shape=jax.ShapeDtypeStruct((B, S, D_CONV), BF16),
        scratch_shapes=[pltpu.VMEM((ts + 2 * CONV_HALO, D_CONV), F32),
                        pltpu.VMEM((SUBLANES, ts + 2 * CONV_HALO - SUBLANES, D_CONV), F32)],
        compiler_params=_cp("parallel", "parallel"),
        name="conformer_conv",
    )(h, h, h, h, h, h, w_b, conv_b.reshape(1, -1), ln_g.reshape(1, -1), ln_b.reshape(1, -1))


def _rope_tables(S):
    half = ROT_DIM // 2
    inv_freq = jnp.power(ROPE_THETA, -jnp.arange(0, ROT_DIM, 2, dtype=F32) / ROT_DIM)
    ang = jnp.arange(S, dtype=F32)[:, None] * inv_freq[None, :]
    cos, sin = jnp.cos(ang), jnp.sin(ang)
    zeros = jnp.zeros((S, ATT_HEAD_DIM - ROT_DIM), F32)
    z8 = jnp.zeros((S, half), F32)
    c_head = jnp.concatenate([cos, cos, zeros + 1.0], axis=1)
    up_head = jnp.concatenate([-sin, z8, zeros], axis=1)
    dn_head = jnp.concatenate([z8, sin, zeros], axis=1)
    tabs = jnp.stack([jnp.tile(t, (1, LANES // ATT_HEAD_DIM)) for t in (c_head, up_head, dn_head)])
    q_scale = ATT_HEAD_DIM ** -0.5
    return jnp.stack([tabs * q_scale, tabs])


def _rope_kernel(x_ref, t_ref, o_ref):
    x = x_ref[...].astype(F32)
    up = pltpu.roll(x, LANES - ROT_DIM // 2, 1)
    dn = pltpu.roll(x, ROT_DIM // 2, 1)
    o_ref[...] = (x * t_ref[0] + up * t_ref[1] + dn * t_ref[2]).astype(o_ref.dtype)


def _rope(h, tabs, ts):
    B, S, _ = h.shape
    nblk = 2 * N_PAIRS
    return pl.pallas_call(
        _rope_kernel,
        grid=(B, S // ts, nblk),
        in_specs=[pl.BlockSpec((None, ts, LANES), lambda b, i, j: (b, i, COL_Q + j)),
                  pl.BlockSpec((None, 3, ts, LANES), lambda b, i, j: (j // N_PAIRS, 0, i, 0))],
        out_specs=pl.BlockSpec((None, ts, LANES), lambda b, i, j: (b, i, j)),
        out_shape=jax.ShapeDtypeStruct((B, S, 2 * D_ATT), BF16),
        compiler_params=_cp("parallel", "parallel", "parallel"),
        name="rope",
    )(h, tabs)


ATT_TQ = 128
ATT_TK = ATT_TQ + 2 * N_SIDE
ATT_HALO = N_SIDE * DILATIONS[-1]


def _att_kernel(q_ref, kp_ref, km_ref, kn_ref, vp_ref, vm_ref, vn_ref, o_ref,
                qf, kf, vf, acc, den, ma, mb, *, C, S):
    c = pl.program_id(2)
    H = ATT_HALO
    qf[...] = q_ref[...].astype(F32)
    kf[pl.ds(0, H), :] = kp_ref[...].astype(F32)
    kf[pl.ds(H, C), :] = km_ref[...].astype(F32)
    kf[pl.ds(H + C, H), :] = kn_ref[...].astype(F32)
    vf[pl.ds(0, H), :] = vp_ref[...].astype(F32)
    vf[pl.ds(H, C), :] = vm_ref[...].astype(F32)
    vf[pl.ds(H + C, H), :] = vn_ref[...].astype(F32)
    acc[...] = jnp.zeros_like(acc)
    den[...] = jnp.zeros_like(den)
    ma[...] = jnp.full_like(ma, NEG_INF)
    mb[...] = jnp.full_like(mb, NEG_INF)

    lane = lax.broadcasted_iota(jnp.int32, (ATT_TQ, LANES), 1)
    first_head = lane < ATT_HEAD_DIM
    qi = lax.broadcasted_iota(jnp.int32, (ATT_TQ, ATT_TK), 0)
    kj = lax.broadcasted_iota(jnp.int32, (ATT_TQ, ATT_TK), 1)
    band = jnp.abs(kj - qi - N_SIDE) <= N_SIDE
    ones = jnp.ones((ATT_TK, LANES), BF16)

    for d in DILATIONS:
        n_blk = C // (d * ATT_TQ)
        seq_len = S // d

        def block(t, carry, d=d, n_blk=n_blk, seq_len=seq_len):
            r = t // n_blk
            ib = t % n_blk
            q0 = r + d * ib * ATT_TQ
            k0 = H + r + d * (ib * ATT_TQ - N_SIDE)
            if d == 1:
                rows_q = pl.ds(q0, ATT_TQ)
                rows_k = pl.ds(k0, ATT_TK)
            else:
                rows_q = pl.ds(q0, ATT_TQ, stride=d)
                rows_k = pl.ds(k0, ATT_TK, stride=d)
            qb = qf[rows_q, :]
            kb = kf[rows_k, :].astype(BF16)
            vb = jnp.concatenate([vf[rows_k, :].astype(BF16), ones], axis=1)
            g0 = c * (C // d) + ib * ATT_TQ - N_SIDE
            kg = kj + g0
            valid = band & (kg >= 0) & (kg < seq_len)

            def head(q_h, m_ref):
                s = lax.dot_general(q_h.astype(BF16), kb, (((1,), (1,)), ((), ())),
                                    preferred_element_type=F32)
                s = jnp.where(valid, s, NEG_INF)
                m_old = m_ref[rows_q, :]
                m_new = jnp.maximum(m_old, jnp.max(s, axis=-1, keepdims=True))
                m_ref[rows_q, :] = m_new
                p = jnp.exp(s - jnp.concatenate([m_new, m_new], axis=1))
                pv = jnp.dot(p.astype(BF16), vb, preferred_element_type=F32)
                return pv, jnp.exp(m_old - m_new)

            pv_a, sc_a = head(jnp.where(first_head, qb, 0.0), ma)
            pv_b, sc_b = head(jnp.where(first_head, 0.0, qb), mb)
            scale = jnp.where(first_head, sc_a, sc_b)
            num = jnp.where(first_head, pv_a[:, :LANES], pv_b[:, :LANES])
            rsum = jnp.where(first_head, pv_a[:, LANES:], pv_b[:, LANES:])
            acc[rows_q, :] = acc[rows_q, :] * scale + num
            den[rows_q, :] = den[rows_q, :] * scale + rsum
            return carry

        lax.fori_loop(0, d * n_blk, block, 0)

    def finish(t, carry):
        r0 = pl.multiple_of(t * ATT_TQ, ATT_TQ)
        rows = pl.ds(r0, ATT_TQ)
        o_ref[rows, :] = (acc[rows, :] / den[rows, :]).astype(o_ref.dtype)
        return carry

    lax.fori_loop(0, C // ATT_TQ, finish, 0)


def _dilated_attention(qk, h, C):
    B, S, _ = qk.shape
    H = ATT_HALO
    nh = C // H
    n_halo = S // H

    def main(col0):
        return pl.BlockSpec((None, C, LANES), lambda b, p, c: (b, c, col0 + p))

    def prev(col0):
        return pl.BlockSpec((None, H, LANES),
                            lambda b, p, c: (b, jnp.maximum(c * nh - 1, 0), col0 + p))

    def nxt(col0):
        return pl.BlockSpec((None, H, LANES),
                            lambda b, p, c: (b, jnp.minimum((c + 1) * nh, n_halo - 1), col0 + p))

    stat = pltpu.VMEM((C, LANES), F32)
    window = pltpu.VMEM((C + 2 * H, LANES), F32)
    return pl.pallas_call(
        functools.partial(_att_kernel, C=C, S=S),
        grid=(B, N_PAIRS, S // C),
        in_specs=[main(0), prev(N_PAIRS), main(N_PAIRS), nxt(N_PAIRS),
                  prev(COL_V), main(COL_V), nxt(COL_V)],
        out_specs=pl.BlockSpec((None, C, LANES), lambda b, p, c: (b, c, p)),
        out_shape=jax.ShapeDtypeStruct((B, S, D_ATT), BF16),
        scratch_shapes=[stat, window, window, stat, stat, stat, stat],
        compiler_params=_cp("parallel", "parallel", "parallel"),
        name="dilated_attention",
    )(qk, qk, qk, qk, h, h, h)


def _mem_att_kernel(q_ref, k_ref, v_ref, o_ref):
    s = lax.dot_general(q_ref[...], k_ref[...], (((1,), (1,)), ((), ())),
                        preferred_element_type=F32) * (MEM_HEAD_DIM ** -0.5)
    p = jnp.exp(s - jnp.max(s, axis=-1, keepdims=True))
    vb = jnp.concatenate([v_ref[...], jnp.ones((N_MEM_TOKENS, LANES), BF16)], axis=1)
    pv = jnp.dot(p.astype(BF16), vb, preferred_element_type=F32)
    o_ref[...] = (pv[:, :LANES] / pv[:, LANES:]).astype(o_ref.dtype)


def _memory_attention(h, kv, ts):
    B, S, _ = h.shape
    return pl.pallas_call(
        _mem_att_kernel,
        grid=(B, N_MEM_HEADS, S // ts),
        in_specs=[pl.BlockSpec((None, ts, LANES), lambda b, hd, i: (b, i, COL_QM + hd)),
                  pl.BlockSpec((None, N_MEM_TOKENS, LANES), lambda b, hd, i: (b, 0, hd)),
                  pl.BlockSpec((None, N_MEM_TOKENS, LANES), lambda b, hd, i: (b, 0, N_MEM_HEADS + hd))],
        out_specs=pl.BlockSpec((None, ts, LANES), lambda b, hd, i: (b, i, hd)),
        out_shape=jax.ShapeDtypeStruct((B, S, D_MEM), BF16),
        compiler_params=_cp("parallel", "parallel", "parallel"),
        name="memory_attention",
    )(h, kv, kv)


def _out_proj_kernel(c_ref, a_ref, m_ref, x_ref, w_ref, g_ref, b_ref, of_ref, ob_ref):
    y = jnp.dot(c_ref[...], w_ref[pl.ds(0, D_CONV), :], preferred_element_type=F32)
    y = y + jnp.dot(a_ref[...], w_ref[pl.ds(D_CONV, D_ATT), :], preferred_element_type=F32)
    y = y + jnp.dot(m_ref[...], w_ref[pl.ds(D_CONV + D_ATT, D_MEM), :], preferred_element_type=F32)
    z = _layernorm_rows(ALPHA * x_ref[...] + y, g_ref[...], b_ref[...])
    of_ref[...] = z
    ob_ref[...] = z.astype(BF16)


def _out_proj_ln(conv_o, att_o, mem_o, x, w_out, g, b, tm):
    T = x.shape[0]
    row = lambda w: pl.BlockSpec((tm, w), lambda i: (i, 0))
    vec = pl.BlockSpec((1, D_MODEL), lambda i: (0, 0))
    return pl.pallas_call(
        _out_proj_kernel,
        grid=(T // tm,),
        in_specs=[row(D_CONV), row(D_ATT), row(D_MEM), row(D_MODEL),
                  pl.BlockSpec((D_MODEL, D_MODEL), lambda i: (0, 0)), vec, vec],
        out_specs=[row(D_MODEL), row(D_MODEL)],
        out_shape=[jax.ShapeDtypeStruct((T, D_MODEL), F32), jax.ShapeDtypeStruct((T, D_MODEL), BF16)],
        compiler_params=_cp("parallel"),
        name="out_proj_ln",
    )(conv_o, att_o, mem_o, x, w_out, g.reshape(1, -1), b.reshape(1, -1))


def _swiglu_step(xb, w1_ref, w3_ref, w2_ref):
    h1 = jnp.dot(xb, w1_ref[...], preferred_element_type=F32)
    h3 = jnp.dot(xb, w3_ref[...], preferred_element_type=F32)
    act = (h1 * jax.nn.sigmoid(h1) * h3).astype(BF16)
    return jnp.dot(act, w2_ref[...], preferred_element_type=F32)


def _ffn_kernel(xb_ref, w1_ref, w3_ref, w2_ref, x_ref, g_ref, b_ref, of_ref, ob_ref, acc):
    f = pl.program_id(1)

    @pl.when(f == 0)
    def _():
        acc[...] = jnp.zeros_like(acc)

    acc[...] += _swiglu_step(xb_ref[...], w1_ref, w3_ref, w2_ref)

    @pl.when(f == pl.num_programs(1) - 1)
    def _():
        z = _layernorm_rows(ALPHA * x_ref[...] + acc[...], g_ref[...], b_ref[...])
        of_ref[...] = z
        ob_ref[...] = z.astype(BF16)


def _ffn_ln(xb, x, w1, w3, w2, g, b, tm, tf):
    T = x.shape[0]
    F = w1.shape[1]
    row = pl.BlockSpec((tm, D_MODEL), lambda i, f: (i, 0))
    vec = pl.BlockSpec((1, D_MODEL), lambda i, f: (0, 0))
    return pl.pallas_call(
        _ffn_kernel,
        grid=(T // tm, F // tf),
        in_specs=[row,
                  pl.BlockSpec((D_MODEL, tf), lambda i, f: (0, f)),
                  pl.BlockSpec((D_MODEL, tf), lambda i, f: (0, f)),
                  pl.BlockSpec((tf, D_MODEL), lambda i, f: (f, 0)),
                  row, vec, vec],
        out_specs=[row, row],
        out_shape=[jax.ShapeDtypeStruct((T, D_MODEL), F32), jax.ShapeDtypeStruct((T, D_MODEL), BF16)],
        scratch_shapes=[pltpu.VMEM((tm, D_MODEL), F32)],
        compiler_params=_cp("parallel", "arbitrary"),
        name="dense_ffn_ln",
    )(xb, w1, w3, w2, x, g.reshape(1, -1), b.reshape(1, -1))


def _split_bf16(x):
    hi = x.astype(BF16)
    return hi, (x - hi.astype(F32)).astype(BF16)


def _router_kernel(x_ref, r_ref, idx_ref, gate_ref):
    xh, xl = _split_bf16(x_ref[...])
    rh, rl = _split_bf16(r_ref[...])
    dot = functools.partial(jnp.dot, preferred_element_type=F32)
    logits = dot(xh, rh) + (dot(xl, rh) + dot(xh, rl))
    e = lax.broadcasted_iota(jnp.int32, logits.shape, 1)
    v1 = jnp.max(logits, axis=-1, keepdims=True)
    i1 = jnp.min(jnp.where(logits == v1, e, N_EXPERTS), axis=-1, keepdims=True)
    rest = jnp.where(e == i1, -jnp.inf, logits)
    v2 = jnp.max(rest, axis=-1, keepdims=True)
    i2 = jnp.min(jnp.where(rest == v2, e, N_EXPERTS), axis=-1, keepdims=True)
    e2 = jnp.exp(v2 - v1)
    denom = 1.0 + e2
    idx_ref[...] = jnp.concatenate([i1, i2], axis=1)
    gate_ref[...] = jnp.concatenate([1.0 / denom, e2 / denom], axis=1)


def _router(x, router, tm):
    T = x.shape[0]
    return pl.pallas_call(
        _router_kernel,
        grid=(T // tm,),
        in_specs=[pl.BlockSpec((tm, D_MODEL), lambda i: (i, 0)),
                  pl.BlockSpec((D_MODEL, N_EXPERTS), lambda i: (0, 0))],
        out_specs=[pl.BlockSpec((tm, 2), lambda i: (i, 0)), pl.BlockSpec((tm, 2), lambda i: (i, 0))],
        out_shape=[jax.ShapeDtypeStruct((T, 2), jnp.int32), jax.ShapeDtypeStruct((T, 2), F32)],
        compiler_params=_cp("parallel"),
        name="router_top2",
    )(x, router)


def _routing_plan(idx, tm):
    T = idx.shape[0]
    n_tiles = 2 * T // tm + N_EXPERTS
    flat = idx.reshape(-1)
    onehot = (flat[:, None] == jnp.arange(N_EXPERTS, dtype=jnp.int32)[None, :]).astype(jnp.int32)
    rank = jnp.cumsum(onehot, axis=0) - onehot
    counts = jnp.sum(onehot, axis=0)
    padded = (counts + tm - 1) // tm * tm
    ends = jnp.cumsum(padded)
    starts = ends - padded
    pos = jnp.sum((rank + starts[None, :]) * onehot, axis=1).astype(jnp.int32)
    tile_row0 = jnp.arange(n_tiles, dtype=jnp.int32) * tm
    tile_expert = jnp.minimum(jnp.sum(tile_row0[:, None] >= ends[None, :], axis=1), N_EXPERTS - 1)
    tile_valid = (tile_row0 < ends[-1]).astype(jnp.int32)
    return pos, tile_expert.astype(jnp.int32), tile_valid, n_tiles


def _dispatch_kernel(pos_ref, x_hbm, zero_hbm, o_hbm, sem, *, td):
    del zero_hbm
    i = pl.program_id(0)

    def copy(j, k):
        return pltpu.make_async_copy(x_hbm.at[pl.ds(i * td + j, 1)],
                                     o_hbm.at[pl.ds(pos_ref[0, 2 * j + k], 1)], sem)

    def start(j, carry):
        copy(j, 0).start()
        copy(j, 1).start()
        return carry

    def wait(j, carry):
        copy(j, 0).wait()
        copy(j, 1).wait()
        return carry

    lax.fori_loop(0, td, start, 0)
    lax.fori_loop(0, td, wait, 0)


def _dispatch(x, pos, n_rows, td):
    T = x.shape[0]
    return pl.pallas_call(
        functools.partial(_dispatch_kernel, td=td),
        grid=(T // td,),
        in_specs=[pl.BlockSpec((None, 1, 2 * td), lambda i: (i, 0, 0), memory_space=pltpu.SMEM),
                  pl.BlockSpec(memory_space=pl.ANY),
                  pl.BlockSpec(memory_space=pl.ANY)],
        out_specs=pl.BlockSpec(memory_space=pl.ANY),
        out_shape=jax.ShapeDtypeStruct((n_rows, D_MODEL), F32),
        scratch_shapes=[pltpu.SemaphoreType.DMA],
        input_output_aliases={2: 0},
        compiler_params=_cp("arbitrary"),
        name="expert_dispatch",
    )(pos.reshape(T // td, 1, 2 * td), x, jnp.zeros((n_rows, D_MODEL), F32))


def _moe_ffn_kernel(te_ref, tv_ref, x_ref, w1_ref, w3_ref, w2_ref, o_ref, xb, acc):
    i = pl.program_id(0)
    f = pl.program_id(1)
    valid = tv_ref[i] > 0

    @pl.when(valid & (f == 0))
    def _():
        xb[...] = x_ref[...].astype(BF16)
        acc[...] = jnp.zeros_like(acc)

    @pl.when(valid)
    def _():
        acc[...] += _swiglu_step(xb[...], w1_ref, w3_ref, w2_ref)

    last = f == pl.num_programs(1) - 1

    @pl.when(valid & last)
    def _():
        o_ref[...] = acc[...]

    @pl.when(jnp.logical_not(valid) & last)
    def _():
        o_ref[...] = jnp.zeros_like(o_ref)


def _moe_ffn(xs, tile_expert, tile_valid, w1, w3, w2, tm, tf):
    n_rows = xs.shape[0]
    F = w1.shape[2]
    nf = F // tf

    def fidx(i, f, tv):
        return jnp.where(tv[i] > 0, f, nf - 1)

    row = pl.BlockSpec((tm, D_MODEL), lambda i, f, te, tv: (i, 0))
    grid_spec = pltpu.PrefetchScalarGridSpec(
        num_scalar_prefetch=2,
        grid=(n_rows // tm, nf),
        in_specs=[row,
                  pl.BlockSpec((None, D_MODEL, tf), lambda i, f, te, tv: (te[i], 0, fidx(i, f, tv))),
                  pl.BlockSpec((None, D_MODEL, tf), lambda i, f, te, tv: (te[i], 0, fidx(i, f, tv))),
                  pl.BlockSpec((None, tf, D_MODEL), lambda i, f, te, tv: (te[i], fidx(i, f, tv), 0))],
        out_specs=row,
        scratch_shapes=[pltpu.VMEM((tm, D_MODEL), BF16), pltpu.VMEM((tm, D_MODEL), F32)],
    )
    return pl.pallas_call(
        _moe_ffn_kernel,
        grid_spec=grid_spec,
        out_shape=jax.ShapeDtypeStruct((n_rows, D_MODEL), F32),
        compiler_params=_cp("parallel", "arbitrary"),
        name="expert_ffn",
    )(tile_expert, tile_valid, xs, w1, w3, w2)


def _combine_kernel(pos_ref, gate_ref, x_ref, g_ref, b_ref, y_hbm, of_ref, ob_ref, ybuf, sem, *, tc):
    def copy(j, k):
        return pltpu.make_async_copy(y_hbm.at[pl.ds(pos_ref[0, 2 * j + k], 1)],
                                     ybuf.at[k, pl.ds(j, 1)], sem)

    def start(j, carry):
        copy(j, 0).start()
        copy(j, 1).start()
        return carry

    def wait(j, carry):
        copy(j, 0).wait()
        copy(j, 1).wait()
        return carry

    lax.fori_loop(0, tc, start, 0)
    lax.fori_loop(0, tc, wait, 0)
    gate = gate_ref[...]
    f = gate[:, 0:1] * ybuf[0] + gate[:, 1:2] * ybuf[1]
    z = _layernorm_rows(ALPHA * x_ref[...] + f, g_ref[...], b_ref[...])
    of_ref[...] = z
    ob_ref[...] = z.astype(BF16)


def _combine_ln(ys, pos, gates, x, g, b, tc):
    T = x.shape[0]
    row = pl.BlockSpec((tc, D_MODEL), lambda i: (i, 0))
    vec = pl.BlockSpec((1, D_MODEL), lambda i: (0, 0))
    return pl.pallas_call(
        functools.partial(_combine_kernel, tc=tc),
        grid=(T // tc,),
        in_specs=[pl.BlockSpec((None, 1, 2 * tc), lambda i: (i, 0, 0), memory_space=pltpu.SMEM),
                  pl.BlockSpec((tc, 2), lambda i: (i, 0)),
                  row, vec, vec,
                  pl.BlockSpec(memory_space=pl.ANY)],
        out_specs=[row, row],
        out_shape=[jax.ShapeDtypeStruct((T, D_MODEL), F32), jax.ShapeDtypeStruct((T, D_MODEL), BF16)],
        scratch_shapes=[pltpu.VMEM((2, tc, D_MODEL), F32), pltpu.SemaphoreType.DMA],
        compiler_params=_cp("arbitrary"),
        name="expert_combine_ln",
    )(pos.reshape(T // tc, 1, 2 * tc), gates, x, g.reshape(1, -1), b.reshape(1, -1), ys)


def _tiles(B, S):
    T = B * S
    return dict(
        tm_proj=min(512, T), tm_kv=min(512, B * N_MEM_TOKENS),
        ts_conv=min(256, S), ts_rope=min(1024, S), att_chunk=min(4096, S), ts_mem=min(1024, S),
        tm_out=min(256, T), tm_ffn=min(512, T), tf=512,
        tm_route=min(512, T), td=min(256, T), tm_moe=min(512, T), tc=min(256, T),
    )


def _trunk(x, mem, p, wb):
    B, S, _ = x.shape
    T = B * S
    t = _tiles(B, S)
    if S % (DILATIONS[-1] * ATT_TQ) or S % ATT_HALO:
        raise ValueError("sequence length must be a multiple of 2048")
    rope_tabs = _rope_tables(S)
    x = x.reshape(T, D_MODEL)
    xb = x.astype(BF16)
    mem_b = mem.reshape(B * N_MEM_TOKENS, D_MODEL).astype(BF16)
    for l in range(DEPTH):
        h = _matmul(xb, wb["w_in"][l], t["tm_proj"], D_IN // 2, BF16).reshape(B, S, D_IN)
        kv = _matmul(mem_b, wb["w_mem_kv"][l], t["tm_kv"], 2 * D_MEM, BF16)
        kv = kv.reshape(B, N_MEM_TOKENS, 2 * D_MEM)
        conv_o = _conformer_conv(h, p["conv_w"][l], p["conv_b"][l], p["conv_ln_g"][l],
                                 p["conv_ln_b"][l], t["ts_conv"])
        qk = _rope(h, rope_tabs, t["ts_rope"])
        att_o = _dilated_attention(qk, h, t["att_chunk"])
        mem_o = _memory_attention(h, kv, t["ts_mem"])
        x, xb = _out_proj_ln(conv_o.reshape(T, D_CONV), att_o.reshape(T, D_ATT),
                             mem_o.reshape(T, D_MEM), x, wb["w_out"][l],
                             p["ln1_g"][l], p["ln1_b"][l], t["tm_out"])
        i = l // 2
        if l % 2 == 0:
            x, xb = _ffn_ln(xb, x, wb["ffn_w1"][i], wb["ffn_w3"][i], wb["ffn_w2"][i],
                            p["ln2_g"][l], p["ln2_b"][l], t["tm_ffn"], t["tf"])
        else:
            idx, gates = _router(x, p["moe_router"][i], t["tm_route"])
            pos, tile_expert, tile_valid, n_tiles = _routing_plan(idx, t["tm_moe"])
            xs = _dispatch(x, pos, n_tiles * t["tm_moe"], t["td"])
            ys = _moe_ffn(xs, tile_expert, tile_valid, wb["moe_w1"][i], wb["moe_w3"][i],
                          wb["moe_w2"][i], t["tm_moe"], t["tf"])
            x, xb = _combine_ln(ys, pos, gates, x, p["ln2_g"][l], p["ln2_b"][l], t["tc"])
    return x.reshape(B, S, D_MODEL)


def kernel(x_prompt, x_sample, mem_prompt, mem_sample, w_in, conv_w, conv_b, conv_ln_g, conv_ln_b,
           w_mem_kv, w_out, ln1_g, ln1_b, ffn_w1, ffn_w3, ffn_w2, moe_router, moe_w1, moe_w3, moe_w2,
           ln2_g, ln2_b):
    p = dict(conv_w=conv_w, conv_b=conv_b, conv_ln_g=conv_ln_g, conv_ln_b=conv_ln_b,
             ln1_g=ln1_g, ln1_b=ln1_b, ln2_g=ln2_g, ln2_b=ln2_b, moe_router=moe_router)
    wb = dict(w_in=w_in, w_mem_kv=w_mem_kv, w_out=w_out, ffn_w1=ffn_w1, ffn_w3=ffn_w3, ffn_w2=ffn_w2,
              moe_w1=moe_w1, moe_w3=moe_w3, moe_w2=moe_w2)
    wb = {k: v.astype(BF16) for k, v in wb.items()}
    y_prompt = _trunk(x_prompt, mem_prompt, p, wb)
    y_sample = _trunk(x_sample, mem_sample, p, wb)
    return (y_prompt, y_sample)
```

```python
import functools
import math

import jax
import jax.numpy as jnp
import numpy as np
from jax import lax
from jax.experimental import pallas as pl
from jax.experimental.pallas import tpu as pltpu

F32 = jnp.float32
BF16 = jnp.bfloat16

D_MODEL = 2048
DEPTH = 2
D_CONV = 768
D_ATT = 768
D_MEM = 512
ATT_HEAD_DIM = 64
N_MEM_HEADS = 4
MEM_HEAD_DIM = 128
N_MEM_TOKENS = 256
D_IN = 2 * D_CONV + 3 * D_ATT + D_MEM
CONV_WIDTH = 31
CONV_PAD = CONV_WIDTH // 2
DILATIONS = (1, 4, 16)
N_SIDE = 64
ROT_DIM = 16
ROPE_THETA = 500000.0
N_EXPERTS = 8
ALPHA = (2 * DEPTH) ** 0.25
LN_EPS = 1e-5
NEG_INF = -1e30

LANES = 128
SUBLANES = 8
N_PAIRS = D_ATT // LANES

VMEM_LIMIT = 56 * 1024 * 1024


def _cp(*sem):
    return pltpu.CompilerParams(dimension_semantics=sem, vmem_limit_bytes=VMEM_LIMIT)


def _layernorm_rows(z, g, b):
    mu = jnp.mean(z, axis=-1, keepdims=True)
    zc = z - mu
    var = jnp.mean(zc * zc, axis=-1, keepdims=True)
    return zc * lax.rsqrt(var + LN_EPS) * g + b


def _matmul_kernel(x_ref, w_ref, o_ref):
    o_ref[...] = jnp.dot(x_ref[...].astype(BF16), w_ref[...],
                         preferred_element_type=F32).astype(o_ref.dtype)


def _matmul(x, w, tm, tn, out_dtype):
    M, K = x.shape
    N = w.shape[1]
    return pl.pallas_call(
        _matmul_kernel,
        grid=(N // tn, M // tm),
        in_specs=[pl.BlockSpec((tm, K), lambda j, i: (i, 0)),
                  pl.BlockSpec((K, tn), lambda j, i: (0, j))],
        out_specs=pl.BlockSpec((tm, tn), lambda j, i: (i, j)),
        out_shape=jax.ShapeDtypeStruct((M, N), out_dtype),
        compiler_params=_cp("parallel", "parallel"),
        name="proj_matmul",
    )(x, w)


CONV_HALO = 16
CONV_ROWS = 32


def _conv_kernel(ap_ref, am_ref, an_ref, gp_ref, gm_ref, gn_ref, w_ref, cb_ref, lg_ref, lb_ref,
                 o_ref, u_ref, ush_ref, y_ref, *, ts):
    i = pl.program_id(1)
    last = pl.num_programs(1) - 1

    def glu(a_ref, g_ref):
        return a_ref[...].astype(F32) * jax.nn.sigmoid(g_ref[...].astype(F32))

    u_ref[pl.ds(CONV_HALO, ts), :] = glu(am_ref, gm_ref)
    u_ref[pl.ds(0, CONV_HALO), :] = jnp.where(i > 0, glu(ap_ref, gp_ref), 0.0)
    u_ref[pl.ds(CONV_HALO + ts, CONV_HALO), :] = jnp.where(i < last, glu(an_ref, gn_ref), 0.0)

    cb = cb_ref[...]
    lg = lg_ref[...]
    lb = lb_ref[...]

    def fill(c, carry):
        r = pl.multiple_of(c * SUBLANES, SUBLANES)
        two = u_ref[pl.ds(r, 2 * SUBLANES), :]
        for m in range(SUBLANES):
            ush_ref[m, pl.ds(r, SUBLANES), :] = two[m:m + SUBLANES]
        return carry

    lax.fori_loop(0, (ts + 2 * CONV_HALO) // SUBLANES - 1, fill, 0)

    n_sub = CONV_ROWS // SUBLANES

    def conv(c, carry):
        r0 = pl.multiple_of(c * CONV_ROWS, CONV_ROWS)
        acc = [jnp.zeros((SUBLANES, D_CONV), F32) for _ in range(n_sub)]
        for t in range(CONV_WIDTH):
            q, m = divmod(t + CONV_HALO - CONV_PAD, SUBLANES)
            w_t = w_ref[t]
            for k in range(n_sub):
                acc[k] = acc[k] + w_t * ush_ref[m, pl.ds(r0 + (q + k) * SUBLANES, SUBLANES), :]
        for k in range(n_sub):
            y_ref[pl.ds(r0 + k * SUBLANES, SUBLANES), :] = acc[k] + cb
        return carry

    lax.fori_loop(0, ts // CONV_ROWS, conv, 0)

    def norm(c, carry):
        r0 = pl.multiple_of(c * CONV_HALO, CONV_HALO)
        y = _layernorm_rows(y_ref[pl.ds(r0, CONV_HALO), :], lg, lb)
        o_ref[pl.ds(r0, CONV_HALO), :] = (y * jax.nn.sigmoid(y)).astype(o_ref.dtype)
        return carry

    lax.fori_loop(0, ts // CONV_HALO, norm, 0, unroll=8)


def _conformer_conv(h, conv_w, conv_b, ln_g, ln_b, ts):
    B, S, _ = h.shape
    nh = ts // CONV_HALO
    n_halo = S // CONV_HALO
    w_b = jnp.broadcast_to(conv_w[:, None, :], (CONV_WIDTH, SUBLANES, D_CONV)).astype(F32)

    def main(col):
        return pl.BlockSpec((None, ts, D_CONV), lambda b, i: (b, i, col))

    def prev(col):
        return pl.BlockSpec((None, CONV_HALO, D_CONV),
                            lambda b, i: (b, jnp.maximum(i * nh - 1, 0), col))

    def nxt(col):
        return pl.BlockSpec((None, CONV_HALO, D_CONV),
                            lambda b, i: (b, jnp.minimum((i + 1) * nh, n_halo - 1), col))

    vec = pl.BlockSpec((1, D_CONV), lambda b, i: (0, 0))
    return pl.pallas_call(
        functools.partial(_conv_kernel, ts=ts),
        grid=(B, S // ts),
        in_specs=[prev(0), main(0), nxt(0), prev(1), main(1), nxt(1),
                  pl.BlockSpec((CONV_WIDTH, SUBLANES, D_CONV), lambda b, i: (0, 0, 0)),
                  vec, vec, vec],
        out_specs=pl.BlockSpec((None, ts, D_CONV), lambda b, i: (b, i, 0)),
        out_shape=jax.ShapeDtypeStruct((B, S, D_CONV), BF16),
        scratch_shapes=[pltpu.VMEM((ts + 2 * CONV_HALO, D_CONV), F32),
                        pltpu.VMEM((SUBLANES, ts + 2 * CONV_HALO - SUBLANES, D_CONV), F32),
                        pltpu.VMEM((ts, D_CONV), F32)],
        compiler_params=_cp("parallel", "parallel"),
        name="conformer_conv",
    )(h, h, h, h, h, h, w_b, conv_b.reshape(1, -1), ln_g.reshape(1, -1), ln_b.reshape(1, -1))


def _rope_tables(S):
    half = ROT_DIM // 2
    inv_freq = jnp.power(ROPE_THETA, -jnp.arange(0, ROT_DIM, 2, dtype=F32) / ROT_DIM)
    ang = jnp.arange(S, dtype=F32)[:, None] * inv_freq[None, :]
    cos, sin = jnp.cos(ang), jnp.sin(ang)
    zeros = jnp.zeros((S, ATT_HEAD_DIM - ROT_DIM), F32)
    z8 = jnp.zeros((S, half), F32)
    c_head = jnp.concatenate([cos, cos, zeros + 1.0], axis=1)
    up_head = jnp.concatenate([-sin, z8, zeros], axis=1)
    dn_head = jnp.concatenate([z8, sin, zeros], axis=1)
    tabs = jnp.stack([jnp.tile(t, (1, LANES // ATT_HEAD_DIM)) for t in (c_head, up_head, dn_head)])
    q_scale = ATT_HEAD_DIM ** -0.5 * math.log2(math.e)
    return jnp.stack([tabs * q_scale, tabs])


def _proj_blocks_kernel(x_ref, w_ref, o_ref):
    y = jnp.dot(x_ref[...], w_ref[...], preferred_element_type=F32)
    for j in range(o_ref.shape[0]):
        o_ref[j] = y[:, j * LANES:(j + 1) * LANES].astype(o_ref.dtype)


def _proj_rope_kernel(x_ref, w_ref, t_ref, o_ref):
    y = jnp.dot(x_ref[...], w_ref[...], preferred_element_type=F32)
    for j in range(o_ref.shape[0]):
        x = y[:, j * LANES:(j + 1) * LANES]
        t = t_ref[j // N_PAIRS]
        up = pltpu.roll(x, LANES - ROT_DIM // 2, 1)
        dn = pltpu.roll(x, ROT_DIM // 2, 1)
        o_ref[j] = (x * t[0] + up * t[1] + dn * t[2]).astype(o_ref.dtype)


def _proj_blocks(x, w, tm, rope_tabs=None):
    M, K = x.shape
    N = w.shape[1]
    in_specs = [pl.BlockSpec((tm, K), lambda i: (i, 0)), pl.BlockSpec((K, N), lambda i: (0, 0))]
    args = [x, w]
    body = _proj_blocks_kernel
    if rope_tabs is not None:
        n_pos = rope_tabs.shape[2] // tm
        in_specs.append(pl.BlockSpec((2, 3, tm, LANES), lambda i: (0, 0, i % n_pos, 0)))
        args.append(rope_tabs)
        body = _proj_rope_kernel
    return pl.pallas_call(
        body,
        grid=(M // tm,),
        in_specs=in_specs,
        out_specs=pl.BlockSpec((N // LANES, tm, LANES), lambda i: (0, i, 0)),
        out_shape=jax.ShapeDtypeStruct((N // LANES, M, LANES), BF16),
        compiler_params=_cp("parallel"),
        name="proj_rope" if rope_tabs is not None else "proj_blocks",
    )(*args)


ATT_TQ = 128
ATT_TK = ATT_TQ + 2 * N_SIDE
ATT_HALO = N_SIDE * DILATIONS[-1]
ATT_BLOCKS_PER_STEP = 8


def _att_kernel(q_ref, kp_ref, km_ref, kn_ref, vp_ref, vm_ref, vn_ref, o_ref,
                qf, kf, vf, acc, den, ma, mb, *, C, S):
    c = pl.program_id(2)
    H = ATT_HALO
    qf[...] = q_ref[...].astype(F32)
    kf[pl.ds(0, H), :] = kp_ref[...].astype(F32)
    kf[pl.ds(H, C), :] = km_ref[...].astype(F32)
    kf[pl.ds(H + C, H), :] = kn_ref[...].astype(F32)
    vf[pl.ds(0, H), :] = vp_ref[...].astype(F32)
    vf[pl.ds(H, C), :] = vm_ref[...].astype(F32)
    vf[pl.ds(H + C, H), :] = vn_ref[...].astype(F32)

    lane = lax.broadcasted_iota(jnp.int32, (ATT_TQ, LANES), 1)
    first_head = lane < ATT_HEAD_DIM
    qi = lax.broadcasted_iota(jnp.int32, (ATT_TQ, ATT_TK), 0)
    kj = lax.broadcasted_iota(jnp.int32, (ATT_TQ, ATT_TK), 1)
    band = jnp.abs(kj - qi - N_SIDE) <= N_SIDE
    kcol = lax.broadcasted_iota(jnp.int32, (1, ATT_TK), 1)
    ones = jnp.ones((ATT_TK, LANES), BF16)

    for d in reversed(DILATIONS):
        n_blk = C // (d * ATT_TQ)
        seq_len = S // d
        init = d == DILATIONS[-1]

        def rows_of(t, d=d, n_blk=n_blk):
            r = t // n_blk
            ib = t % n_blk
            q0 = r + d * ib * ATT_TQ
            k0 = H + r + d * (ib * ATT_TQ - N_SIDE)
            if d == 1:
                return ib, pl.ds(q0, ATT_TQ), pl.ds(k0, ATT_TK)
            return ib, pl.ds(q0, ATT_TQ, stride=d), pl.ds(k0, ATT_TK, stride=d)

        def load(t, init=init):
            ib, rows_q, rows_k = rows_of(t)
            x = dict(ib=ib, q=qf[rows_q, :], k=kf[rows_k, :], v=vf[rows_k, :])
            if not init:
                x.update(ma=ma[rows_q, :], mb=mb[rows_q, :], acc=acc[rows_q, :], den=den[rows_q, :])
            return x

        def compute(x, d=d, seq_len=seq_len, init=init):
            kb = x["k"].astype(BF16)
            vb = jnp.concatenate([x["v"].astype(BF16), ones], axis=1)
            kg = kcol + (c * (C // d) + x["ib"] * ATT_TQ - N_SIDE)
            col_bias = jnp.where((kg >= 0) & (kg < seq_len), 0.0, NEG_INF)

            def head(q_h, m_old):
                s = lax.dot_general(q_h.astype(BF16), kb, (((1,), (1,)), ((), ())),
                                    preferred_element_type=F32)
                s = jnp.where(band, s, NEG_INF) + col_bias
                m_new = jnp.max(s, axis=-1, keepdims=True)
                if m_old is not None:
                    m_new = jnp.maximum(m_old, m_new)
                else:
                    m_new = jnp.broadcast_to(m_new, (ATT_TQ, LANES))
                p = jnp.exp2(s - jnp.concatenate([m_new, m_new], axis=1))
                pv = jnp.dot(p.astype(BF16), vb, preferred_element_type=F32)
                return pv, m_new

            pv_a, ma_new = head(jnp.where(first_head, x["q"], 0.0), x.get("ma"))
            pv_b, mb_new = head(jnp.where(first_head, 0.0, x["q"]), x.get("mb"))
            num = jnp.where(first_head, pv_a[:, :LANES], pv_b[:, :LANES])
            rsum = jnp.where(first_head, pv_a[:, LANES:], pv_b[:, LANES:])
            if init:
                return dict(ma=ma_new, mb=mb_new, acc=num, den=rsum)
            scale = jnp.where(first_head, jnp.exp2(x["ma"] - ma_new), jnp.exp2(x["mb"] - mb_new))
            return dict(ma=ma_new, mb=mb_new, acc=x["acc"] * scale + num, den=x["den"] * scale + rsum)

        def store(t, y):
            _, rows_q, _ = rows_of(t)
            ma[rows_q, :] = y["ma"]
            mb[rows_q, :] = y["mb"]
            acc[rows_q, :] = y["acc"]
            den[rows_q, :] = y["den"]

        def step(i, carry):
            ts = [i * ATT_BLOCKS_PER_STEP + u for u in range(ATT_BLOCKS_PER_STEP)]
            ys = [compute(x) for x in [load(t) for t in ts]]
            for t, y in zip(ts, ys):
                store(t, y)
            return carry

        lax.fori_loop(0, d * n_blk // ATT_BLOCKS_PER_STEP, step, 0)

    def finish(t, carry):
        r0 = pl.multiple_of(t * ATT_TQ, ATT_TQ)
        rows = pl.ds(r0, ATT_TQ)
        o_ref[rows, :] = (acc[rows, :] / den[rows, :]).astype(o_ref.dtype)
        return carry

    lax.fori_loop(0, C // ATT_TQ, finish, 0)


def _dilated_attention(qk, vm, B, S, C):
    H = ATT_HALO
    n_chunk = S // C
    nh = C // H
    n_halo = S // H

    def main(blk0):
        return pl.BlockSpec((None, C, LANES), lambda b, p, c: (blk0 + p, b * n_chunk + c, 0))

    def prev(blk0):
        return pl.BlockSpec((None, H, LANES),
                            lambda b, p, c: (blk0 + p, b * n_halo + jnp.maximum(c * nh - 1, 0), 0))

    def nxt(blk0):
        return pl.BlockSpec((None, H, LANES),
                            lambda b, p, c: (blk0 + p, b * n_halo + jnp.minimum((c + 1) * nh, n_halo - 1), 0))

    stat = pltpu.VMEM((C, LANES), F32)
    window = pltpu.VMEM((C + 2 * H, LANES), F32)
    return pl.pallas_call(
        functools.partial(_att_kernel, C=C, S=S),
        grid=(B, N_PAIRS, n_chunk),
        in_specs=[main(0), prev(N_PAIRS), main(N_PAIRS), nxt(N_PAIRS), prev(0), main(0), nxt(0)],
        out_specs=main(0),
        out_shape=jax.ShapeDtypeStruct((N_PAIRS, B * S, LANES), BF16),
        scratch_shapes=[stat, window, window, stat, stat, stat, stat],
        compiler_params=_cp("parallel", "parallel", "parallel"),
        name="dilated_attention",
    )(qk, qk, qk, qk, vm, vm, vm)


def _mem_att_kernel(q_ref, k_ref, v_ref, o_ref):
    s = lax.dot_general(q_ref[...], k_ref[...], (((1,), (1,)), ((), ())),
                        preferred_element_type=F32) * (MEM_HEAD_DIM ** -0.5)
    p = jnp.exp(s - jnp.max(s, axis=-1, keepdims=True))
    vb = jnp.concatenate([v_ref[...], jnp.ones((N_MEM_TOKENS, LANES), BF16)], axis=1)
    pv = jnp.dot(p.astype(BF16), vb, preferred_element_type=F32)
    o_ref[...] = (pv[:, :LANES] / pv[:, LANES:]).astype(o_ref.dtype)


def _memory_attention(vm, kv, B, S, ts):
    n_t = S // ts
    return pl.pallas_call(
        _mem_att_kernel,
        grid=(B, N_MEM_HEADS, n_t),
        in_specs=[pl.BlockSpec((None, ts, LANES), lambda b, hd, i: (N_PAIRS + hd, b * n_t + i, 0)),
                  pl.BlockSpec((None, N_MEM_TOKENS, LANES), lambda b, hd, i: (b, 0, hd)),
                  pl.BlockSpec((None, N_MEM_TOKENS, LANES), lambda b, hd, i: (b, 0, N_MEM_HEADS + hd))],
        out_specs=pl.BlockSpec((None, ts, LANES), lambda b, hd, i: (hd, b * n_t + i, 0)),
        out_shape=jax.ShapeDtypeStruct((N_MEM_HEADS, B * S, LANES), BF16),
        compiler_params=_cp("parallel", "parallel", "parallel"),
        name="memory_attention",
    )(vm, kv, kv)


def _out_proj_kernel(c_ref, a_ref, m_ref, x_ref, w_ref, g_ref, b_ref, of_ref, ob_ref):
    mixed = jnp.concatenate([c_ref[...]] + [a_ref[j] for j in range(N_PAIRS)]
                            + [m_ref[j] for j in range(N_MEM_HEADS)], axis=1)
    y = jnp.dot(mixed, w_ref[...], preferred_element_type=F32)
    z = _layernorm_rows(ALPHA * x_ref[...] + y, g_ref[...], b_ref[...])
    of_ref[...] = z
    ob_ref[...] = z.astype(BF16)


def _out_proj_ln(conv_o, att_o, mem_o, x, w_out, g, b, tm):
    T = x.shape[0]
    row = lambda w: pl.BlockSpec((tm, w), lambda i: (i, 0))
    slabs = lambda n: pl.BlockSpec((n, tm, LANES), lambda i: (0, i, 0))
    vec = pl.BlockSpec((1, D_MODEL), lambda i: (0, 0))
    return pl.pallas_call(
        _out_proj_kernel,
        grid=(T // tm,),
        in_specs=[row(D_CONV), slabs(N_PAIRS), slabs(N_MEM_HEADS), row(D_MODEL),
                  pl.BlockSpec((D_MODEL, D_MODEL), lambda i: (0, 0), pipeline_mode=pl.Buffered(1)),
                  vec, vec],
        out_specs=[row(D_MODEL), row(D_MODEL)],
        out_shape=[jax.ShapeDtypeStruct((T, D_MODEL), F32), jax.ShapeDtypeStruct((T, D_MODEL), BF16)],
        compiler_params=_cp("parallel"),
        name="out_proj_ln",
    )(conv_o, att_o, mem_o, x, w_out, g.reshape(1, -1), b.reshape(1, -1))


def _swiglu_step(xb, w1_ref, w3_ref, w2_ref):
    h1 = jnp.dot(xb, w1_ref[...], preferred_element_type=F32)
    h3 = jnp.dot(xb, w3_ref[...], preferred_element_type=F32)
    act = (h1 * jax.nn.sigmoid(h1) * h3).astype(BF16)
    return jnp.dot(act, w2_ref[...], preferred_element_type=F32)


def _ffn_kernel(xb_ref, w1_ref, w3_ref, w2_ref, x_ref, g_ref, b_ref, of_ref, ob_ref, acc):
    f = pl.program_id(1)

    @pl.when(f == 0)
    def _():
        acc[...] = jnp.zeros_like(acc)

    acc[...] += _swiglu_step(xb_ref[...], w1_ref, w3_ref, w2_ref)

    @pl.when(f == pl.num_programs(1) - 1)
    def _():
        z = _layernorm_rows(ALPHA * x_ref[...] + acc[...], g_ref[...], b_ref[...])
        of_ref[...] = z
        ob_ref[...] = z.astype(BF16)


def _ffn_ln(xb, x, w1, w3, w2, g, b, tm, tf):
    T = x.shape[0]
    F = w1.shape[1]
    row = pl.BlockSpec((tm, D_MODEL), lambda i, f: (i, 0))
    vec = pl.BlockSpec((1, D_MODEL), lambda i, f: (0, 0))
    return pl.pallas_call(
        _ffn_kernel,
        grid=(T // tm, F // tf),
        in_specs=[row,
                  pl.BlockSpec((D_MODEL, tf), lambda i, f: (0, f)),
                  pl.BlockSpec((D_MODEL, tf), lambda i, f: (0, f)),
                  pl.BlockSpec((tf, D_MODEL), lambda i, f: (f, 0)),
                  row, vec, vec],
        out_specs=[row, row],
        out_shape=[jax.ShapeDtypeStruct((T, D_MODEL), F32), jax.ShapeDtypeStruct((T, D_MODEL), BF16)],
        scratch_shapes=[pltpu.VMEM((tm, D_MODEL), F32)],
        compiler_params=_cp("parallel", "arbitrary"),
        name="dense_ffn_ln",
    )(xb, w1, w3, w2, x, g.reshape(1, -1), b.reshape(1, -1))


def _split_bf16(x):
    hi = x.astype(BF16)
    return hi, (x - hi.astype(F32)).astype(BF16)


def _router_kernel(x_ref, r_ref, idx_ref, gate_ref):
    xh, xl = _split_bf16(x_ref[...])
    rh, rl = _split_bf16(r_ref[...])
    dot = functools.partial(jnp.dot, preferred_element_type=F32)
    logits = dot(xh, rh) + (dot(xl, rh) + dot(xh, rl))
    e = lax.broadcasted_iota(jnp.int32, logits.shape, 1)
    v1 = jnp.max(logits, axis=-1, keepdims=True)
    i1 = jnp.min(jnp.where(logits == v1, e, N_EXPERTS), axis=-1, keepdims=True)
    rest = jnp.where(e == i1, -jnp.inf, logits)
    v2 = jnp.max(rest, axis=-1, keepdims=True)
    i2 = jnp.min(jnp.where(rest == v2, e, N_EXPERTS), axis=-1, keepdims=True)
    e2 = jnp.exp(v2 - v1)
    denom = 1.0 + e2
    idx_ref[...] = jnp.concatenate([i1, i2], axis=1)
    gate_ref[...] = jnp.concatenate([1.0 / denom, e2 / denom], axis=1)


def _router(x, router, tm):
    T = x.shape[0]
    return pl.pallas_call(
        _router_kernel,
        grid=(T // tm,),
        in_specs=[pl.BlockSpec((tm, D_MODEL), lambda i: (i, 0)),
                  pl.BlockSpec((D_MODEL, N_EXPERTS), lambda i: (0, 0))],
        out_specs=[pl.BlockSpec((tm, 2), lambda i: (i, 0)), pl.BlockSpec((tm, 2), lambda i: (i, 0))],
        out_shape=[jax.ShapeDtypeStruct((T, 2), jnp.int32), jax.ShapeDtypeStruct((T, 2), F32)],
        compiler_params=_cp("parallel"),
        name="router_top2",
    )(x, router)


def _routing_plan(idx, tm):
    T = idx.shape[0]
    i32 = jnp.int32
    n_visits = 2 * T // tm + N_EXPERTS - 1
    flat = idx.reshape(-1)
    onehot = (flat[:, None] == jnp.arange(N_EXPERTS, dtype=i32)[None, :]).astype(i32)
    rank = jnp.cumsum(onehot, axis=0) - onehot
    counts = jnp.sum(onehot, axis=0)
    ends = jnp.cumsum(counts)
    starts = ends - counts
    pos = jnp.sum((rank + starts[None, :]) * onehot, axis=1).astype(i32)
    first_tile = starts // tm
    n_e = jnp.where(counts > 0, (ends - 1) // tm - first_tile + 1, 0)
    v_end = jnp.cumsum(n_e)
    v_start = v_end - n_e
    v = jnp.arange(n_visits, dtype=i32)
    valid = v < v_end[-1]
    vc = jnp.minimum(v, v_end[-1] - 1)
    e = jnp.minimum(jnp.sum(vc[:, None] >= v_end[None, :], axis=1), N_EXPERTS - 1).astype(i32)
    tile = first_tile[e] + (vc - v_start[e])
    lo = jnp.maximum(starts[e], tile * tm) - tile * tm
    hi = jnp.minimum(ends[e], (tile + 1) * tm) - tile * tm
    first = jnp.concatenate([jnp.ones((1,), bool), tile[1:] != tile[:-1]])
    table = [tile, e, lo, jnp.where(valid, hi, lo), first & valid, valid]
    return pos, [a.astype(i32) for a in table]


def _dispatch_kernel(pos_ref, x_ref, o_hbm, sem, *, td):
    def copy(j, k):
        return pltpu.make_async_copy(x_ref.at[pl.ds(j, 1)],
                                     o_hbm.at[pl.ds(pos_ref[0, 2 * j + k], 1)], sem)

    def start(j, carry):
        copy(j, 0).start()
        copy(j, 1).start()
        return carry

    def wait(j, carry):
        copy(j, 0).wait()
        copy(j, 1).wait()
        return carry

    lax.fori_loop(0, td, start, 0, unroll=8)
    lax.fori_loop(0, td, wait, 0, unroll=8)


def _dispatch(x, pos, td):
    T = x.shape[0]
    return pl.pallas_call(
        functools.partial(_dispatch_kernel, td=td),
        grid=(T // td,),
        in_specs=[pl.BlockSpec((None, 1, 2 * td), lambda i: (i, 0, 0), memory_space=pltpu.SMEM),
                  pl.BlockSpec((td, D_MODEL), lambda i: (i, 0))],
        out_specs=pl.BlockSpec(memory_space=pl.ANY),
        out_shape=jax.ShapeDtypeStruct((2 * T, D_MODEL), F32),
        scratch_shapes=[pltpu.SemaphoreType.DMA],
        compiler_params=_cp("arbitrary"),
        name="expert_dispatch",
    )(pos.reshape(T // td, 1, 2 * td), x)


def _moe_ffn_kernel(tile_ref, exp_ref, lo_ref, hi_ref, first_ref, valid_ref,
                    x_ref, w1_ref, w3_ref, w2_ref, o_ref, xb, acc):
    del tile_ref, exp_ref
    v = pl.program_id(0)
    f = pl.program_id(1)
    valid = valid_ref[v] > 0

    @pl.when(valid & (f == 0))
    def _():
        row = lax.broadcasted_iota(jnp.int32, x_ref.shape, 0)
        mine = (row >= lo_ref[v]) & (row < hi_ref[v])
        xb[...] = jnp.where(mine, x_ref[...], 0.0).astype(BF16)

    @pl.when((first_ref[v] > 0) & (f == 0))
    def _():
        acc[...] = jnp.zeros_like(acc)

    @pl.when(valid)
    def _():
        acc[...] += _swiglu_step(xb[...], w1_ref, w3_ref, w2_ref)

    @pl.when(valid & (f == pl.num_programs(1) - 1))
    def _():
        o_ref[...] = acc[...]


def _moe_ffn(xs, table, w1, w3, w2, tm, tf):
    n_rows = xs.shape[0]
    n_visits = table[0].shape[0]
    nf = w1.shape[2] // tf

    def fidx(v, f, valid):
        return jnp.where(valid[v] > 0, f, nf - 1)

    row = pl.BlockSpec((tm, D_MODEL), lambda v, f, tile, *_: (tile[v], 0))
    up = pl.BlockSpec((None, D_MODEL, tf),
                      lambda v, f, tile, e, lo, hi, first, valid: (e[v], 0, fidx(v, f, valid)))
    down = pl.BlockSpec((None, tf, D_MODEL),
                        lambda v, f, tile, e, lo, hi, first, valid: (e[v], fidx(v, f, valid), 0))
    grid_spec = pltpu.PrefetchScalarGridSpec(
        num_scalar_prefetch=len(table),
        grid=(n_visits, nf),
        in_specs=[row, up, up, down],
        out_specs=row,
        scratch_shapes=[pltpu.VMEM((tm, D_MODEL), BF16), pltpu.VMEM((tm, D_MODEL), F32)],
    )
    return pl.pallas_call(
        _moe_ffn_kernel,
        grid_spec=grid_spec,
        out_shape=jax.ShapeDtypeStruct((n_rows, D_MODEL), F32),
        compiler_params=_cp("arbitrary", "arbitrary"),
        name="expert_ffn",
    )(*table, xs, w1, w3, w2)


def _combine_kernel(pos_ref, gate_ref, x_ref, g_ref, b_ref, y_hbm, of_ref, ob_ref, ybuf, sem, *, tc):
    def copy(j, k):
        return pltpu.make_async_copy(y_hbm.at[pl.ds(pos_ref[0, 2 * j + k], 1)],
                                     ybuf.at[k, pl.ds(j, 1)], sem)

    def start(j, carry):
        copy(j, 0).start()
        copy(j, 1).start()
        return carry

    def wait(j, carry):
        copy(j, 0).wait()
        copy(j, 1).wait()
        return carry

    lax.fori_loop(0, tc, start, 0)
    lax.fori_loop(0, tc, wait, 0)
    gate = gate_ref[...]
    f = gate[:, 0:1] * ybuf[0] + gate[:, 1:2] * ybuf[1]
    z = _layernorm_rows(ALPHA * x_ref[...] + f, g_ref[...], b_ref[...])
    of_ref[...] = z
    ob_ref[...] = z.astype(BF16)


def _combine_ln(ys, pos, gates, x, g, b, tc):
    T = x.shape[0]
    row = pl.BlockSpec((tc, D_MODEL), lambda i: (i, 0))
    vec = pl.BlockSpec((1, D_MODEL), lambda i: (0, 0))
    return pl.pallas_call(
        functools.partial(_combine_kernel, tc=tc),
        grid=(T // tc,),
        in_specs=[pl.BlockSpec((None, 1, 2 * tc), lambda i: (i, 0, 0), memory_space=pltpu.SMEM),
                  pl.BlockSpec((tc, 2), lambda i: (i, 0)),
                  row, vec, vec,
                  pl.BlockSpec(memory_space=pl.ANY)],
        out_specs=[row, row],
        out_shape=[jax.ShapeDtypeStruct((T, D_MODEL), F32), jax.ShapeDtypeStruct((T, D_MODEL), BF16)],
        scratch_shapes=[pltpu.VMEM((2, tc, D_MODEL), F32), pltpu.SemaphoreType.DMA],
        compiler_params=_cp("arbitrary"),
        name="expert_combine_ln",
    )(pos.reshape(T // tc, 1, 2 * tc), gates, x, g.reshape(1, -1), b.reshape(1, -1), ys)


def _tiles(B, S):
    T = B * S
    return dict(
        tm_proj=min(512, T), tm_kv=min(512, B * N_MEM_TOKENS),
        ts_conv=min(256, S), att_chunk=min(4096, S), ts_mem=min(1024, S),
        tm_out=min(512, T), tm_ffn=min(512, T), tf=512,
        tm_route=min(512, T), td=min(256, T), tm_moe=min(512, T), tc=min(256, T),
    )


def _trunk(x, mem, p, wb):
    B, S, _ = x.shape
    T = B * S
    t = _tiles(B, S)
    if S % (DILATIONS[-1] * ATT_TQ) or S % ATT_HALO:
        raise ValueError("sequence length must be a multiple of 2048")
    rope_tabs = _rope_tables(S)
    x = x.reshape(T, D_MODEL)
    xb = x.astype(BF16)
    mem_b = mem.reshape(B * N_MEM_TOKENS, D_MODEL).astype(BF16)
    for l in range(DEPTH):
        w_in = wb["w_in"][l]
        ag = _matmul(xb, w_in[:, :2 * D_CONV], t["tm_proj"], 2 * D_CONV, BF16)
        qk = _proj_blocks(xb, w_in[:, 2 * D_CONV:2 * D_CONV + 2 * D_ATT], t["tm_proj"], rope_tabs)
        vm = _proj_blocks(xb, w_in[:, 2 * D_CONV + 2 * D_ATT:], t["tm_proj"])
        kv = _matmul(mem_b, wb["w_mem_kv"][l], t["tm_kv"], 2 * D_MEM, BF16)
        kv = kv.reshape(B, N_MEM_TOKENS, 2 * D_MEM)
        conv_o = _conformer_conv(ag.reshape(B, S, 2 * D_CONV), p["conv_w"][l], p["conv_b"][l],
                                 p["conv_ln_g"][l], p["conv_ln_b"][l], t["ts_conv"])
        att_o = _dilated_attention(qk, vm, B, S, t["att_chunk"])
        mem_o = _memory_attention(vm, kv, B, S, t["ts_mem"])
        x, xb = _out_proj_ln(conv_o.reshape(T, D_CONV), att_o, mem_o, x, wb["w_out"][l],
                             p["ln1_g"][l], p["ln1_b"][l], t["tm_out"])
        i = l // 2
        if l % 2 == 0:
            x, xb = _ffn_ln(xb, x, wb["ffn_w1"][i], wb["ffn_w3"][i], wb["ffn_w2"][i],
                            p["ln2_g"][l], p["ln2_b"][l], t["tm_ffn"], t["tf"])
        else:
            idx, gates = _router(x, p["moe_router"][i], t["tm_route"])
            pos, table = _routing_plan(idx, t["tm_moe"])
            xs = _dispatch(x, pos, t["td"])
            ys = _moe_ffn(xs, table, wb["moe_w1"][i], wb["moe_w3"][i], wb["moe_w2"][i],
                          t["tm_moe"], t["tf"])
            x, xb = _combine_ln(ys, pos, gates, x, p["ln2_g"][l], p["ln2_b"][l], t["tc"])
    return x.reshape(B, S, D_MODEL)


def kernel(x_prompt, x_sample, mem_prompt, mem_sample, w_in, conv_w, conv_b, conv_ln_g, conv_ln_b,
           w_mem_kv, w_out, ln1_g, ln1_b, ffn_w1, ffn_w3, ffn_w2, moe_router, moe_w1, moe_w3, moe_w2,
           ln2_g, ln2_b):
    p = dict(conv_w=conv_w, conv_b=conv_b, conv_ln_g=conv_ln_g, conv_ln_b=conv_ln_b,
             ln1_g=ln1_g, ln1_b=ln1_b, ln2_g=ln2_g, ln2_b=ln2_b, moe_router=moe_router)
    wb = dict(w_in=w_in, w_mem_kv=w_mem_kv, w_out=w_out, ffn_w1=ffn_w1, ffn_w3=ffn_w3, ffn_w2=ffn_w2,
              moe_w1=moe_w1, moe_w3=moe_w3, moe_w2=moe_w2)
    wb = {k: v.astype(BF16) for k, v in wb.items()}
    y_prompt = _trunk(x_prompt, mem_prompt, p, wb)
    y_sample = _trunk(x_sample, mem_sample, p, wb)
    return (y_prompt, y_sample)
```

```python
import functools
import math

import jax
import jax.numpy as jnp
import numpy as np
from jax import lax
from jax.experimental import pallas as pl
from jax.experimental.pallas import tpu as pltpu

F32 = jnp.float32
BF16 = jnp.bfloat16

D_MODEL = 2048
DEPTH = 2
D_CONV = 768
D_ATT = 768
D_MEM = 512
ATT_HEAD_DIM = 64
N_MEM_HEADS = 4
MEM_HEAD_DIM = 128
N_MEM_TOKENS = 256
D_IN = 2 * D_CONV + 3 * D_ATT + D_MEM
CONV_WIDTH = 31
CONV_PAD = CONV_WIDTH // 2
DILATIONS = (1, 4, 16)
N_SIDE = 64
ROT_DIM = 16
ROPE_THETA = 500000.0
N_EXPERTS = 8
ALPHA = (2 * DEPTH) ** 0.25
LN_EPS = 1e-5
NEG_INF = -1e30

LANES = 128
SUBLANES = 8
N_PAIRS = D_ATT // LANES

VMEM_LIMIT = 56 * 1024 * 1024


def _cp(*sem):
    return pltpu.CompilerParams(dimension_semantics=sem, vmem_limit_bytes=VMEM_LIMIT)


def _layernorm_rows(z, g, b):
    mu = jnp.mean(z, axis=-1, keepdims=True)
    zc = z - mu
    var = jnp.mean(zc * zc, axis=-1, keepdims=True)
    return zc * lax.rsqrt(var + LN_EPS) * g + b


def _matmul_kernel(x_ref, w_ref, o_ref):
    o_ref[...] = jnp.dot(x_ref[...].astype(BF16), w_ref[...],
                         preferred_element_type=F32).astype(o_ref.dtype)


def _matmul(x, w, tm, tn, out_dtype):
    M, K = x.shape
    N = w.shape[1]
    return pl.pallas_call(
        _matmul_kernel,
        grid=(N // tn, M // tm),
        in_specs=[pl.BlockSpec((tm, K), lambda j, i: (i, 0)),
                  pl.BlockSpec((K, tn), lambda j, i: (0, j))],
        out_specs=pl.BlockSpec((tm, tn), lambda j, i: (i, j)),
        out_shape=jax.ShapeDtypeStruct((M, N), out_dtype),
        compiler_params=_cp("parallel", "parallel"),
        name="proj_matmul",
    )(x, w)


CONV_HALO = 16
CONV_ROWS = 32


def _conv_kernel(ap_ref, am_ref, an_ref, gp_ref, gm_ref, gn_ref, w_ref, cb_ref, lg_ref, lb_ref,
                 o_ref, u_ref, ush_ref, y_ref, *, ts):
    i = pl.program_id(1)
    last = pl.num_programs(1) - 1

    def glu(a_ref, g_ref):
        return a_ref[...].astype(F32) * jax.nn.sigmoid(g_ref[...].astype(F32))

    u_ref[pl.ds(CONV_HALO, ts), :] = glu(am_ref, gm_ref)
    u_ref[pl.ds(0, CONV_HALO), :] = jnp.where(i > 0, glu(ap_ref, gp_ref), 0.0)
    u_ref[pl.ds(CONV_HALO + ts, CONV_HALO), :] = jnp.where(i < last, glu(an_ref, gn_ref), 0.0)

    cb = cb_ref[...]
    lg = lg_ref[...]
    lb = lb_ref[...]

    def fill(c, carry):
        r = pl.multiple_of(c * SUBLANES, SUBLANES)
        two = u_ref[pl.ds(r, 2 * SUBLANES), :]
        for m in range(SUBLANES):
            ush_ref[m, pl.ds(r, SUBLANES), :] = two[m:m + SUBLANES]
        return carry

    lax.fori_loop(0, (ts + 2 * CONV_HALO) // SUBLANES - 1, fill, 0)

    n_sub = CONV_ROWS // SUBLANES

    def conv(c, carry):
        r0 = pl.multiple_of(c * CONV_ROWS, CONV_ROWS)
        acc = [jnp.zeros((SUBLANES, D_CONV), F32) for _ in range(n_sub)]
        for t in range(CONV_WIDTH):
            q, m = divmod(t + CONV_HALO - CONV_PAD, SUBLANES)
            w_t = w_ref[t]
            for k in range(n_sub):
                acc[k] = acc[k] + w_t * ush_ref[m, pl.ds(r0 + (q + k) * SUBLANES, SUBLANES), :]
        for k in range(n_sub):
            y_ref[pl.ds(r0 + k * SUBLANES, SUBLANES), :] = acc[k] + cb
        return carry

    lax.fori_loop(0, ts // CONV_ROWS, conv, 0)

    def norm(c, carry):
        r0 = pl.multiple_of(c * CONV_HALO, CONV_HALO)
        y = _layernorm_rows(y_ref[pl.ds(r0, CONV_HALO), :], lg, lb)
        o_ref[pl.ds(r0, CONV_HALO), :] = (y * jax.nn.sigmoid(y)).astype(o_ref.dtype)
        return carry

    lax.fori_loop(0, ts // CONV_HALO, norm, 0, unroll=8)


def _conformer_conv(h, conv_w, conv_b, ln_g, ln_b, ts):
    B, S, _ = h.shape
    nh = ts // CONV_HALO
    n_halo = S // CONV_HALO
    w_b = jnp.broadcast_to(conv_w[:, None, :], (CONV_WIDTH, SUBLANES, D_CONV)).astype(F32)

    def main(col):
        return pl.BlockSpec((None, ts, D_CONV), lambda b, i: (b, i, col))

    def prev(col):
        return pl.BlockSpec((None, CONV_HALO, D_CONV),
                            lambda b, i: (b, jnp.maximum(i * nh - 1, 0), col))

    def nxt(col):
        return pl.BlockSpec((None, CONV_HALO, D_CONV),
                            lambda b, i: (b, jnp.minimum((i + 1) * nh, n_halo - 1), col))

    vec = pl.BlockSpec((1, D_CONV), lambda b, i: (0, 0))
    return pl.pallas_call(
        functools.partial(_conv_kernel, ts=ts),
        grid=(B, S // ts),
        in_specs=[prev(0), main(0), nxt(0), prev(1), main(1), nxt(1),
                  pl.BlockSpec((CONV_WIDTH, SUBLANES, D_CONV), lambda b, i: (0, 0, 0)),
                  vec, vec, vec],
        out_specs=pl.BlockSpec((None, ts, D_CONV), lambda b, i: (b, i, 0)),
        out_shape=jax.ShapeDtypeStruct((B, S, D_CONV), BF16),
        scratch_shapes=[pltpu.VMEM((ts + 2 * CONV_HALO, D_CONV), F32),
                        pltpu.VMEM((SUBLANES, ts + 2 * CONV_HALO - SUBLANES, D_CONV), F32),
                        pltpu.VMEM((ts, D_CONV), F32)],
        compiler_params=_cp("parallel", "parallel"),
        name="conformer_conv",
    )(h, h, h, h, h, h, w_b, conv_b.reshape(1, -1), ln_g.reshape(1, -1), ln_b.reshape(1, -1))


def _rope_tables(S):
    half = ROT_DIM // 2
    inv_freq = jnp.power(ROPE_THETA, -jnp.arange(0, ROT_DIM, 2, dtype=F32) / ROT_DIM)
    ang = jnp.arange(S, dtype=F32)[:, None] * inv_freq[None, :]
    cos, sin = jnp.cos(ang), jnp.sin(ang)
    zeros = jnp.zeros((S, ATT_HEAD_DIM - ROT_DIM), F32)
    z8 = jnp.zeros((S, half), F32)
    c_head = jnp.concatenate([cos, cos, zeros + 1.0], axis=1)
    up_head = jnp.concatenate([-sin, z8, zeros], axis=1)
    dn_head = jnp.concatenate([z8, sin, zeros], axis=1)
    tabs = jnp.stack([jnp.tile(t, (1, LANES // ATT_HEAD_DIM)) for t in (c_head, up_head, dn_head)])
    q_scale = ATT_HEAD_DIM ** -0.5 * math.log2(math.e)
    return jnp.stack([tabs * q_scale, tabs])


def _proj_blocks_kernel(x_ref, w_ref, o_ref):
    y = jnp.dot(x_ref[...], w_ref[...], preferred_element_type=F32)
    for j in range(o_ref.shape[0]):
        o_ref[j] = y[:, j * LANES:(j + 1) * LANES].astype(o_ref.dtype)


def _proj_rope_kernel(x_ref, w_ref, t_ref, o_ref):
    y = jnp.dot(x_ref[...], w_ref[...], preferred_element_type=F32)
    for j in range(o_ref.shape[0]):
        x = y[:, j * LANES:(j + 1) * LANES]
        t = t_ref[j // N_PAIRS]
        up = pltpu.roll(x, LANES - ROT_DIM // 2, 1)
        dn = pltpu.roll(x, ROT_DIM // 2, 1)
        o_ref[j] = (x * t[0] + up * t[1] + dn * t[2]).astype(o_ref.dtype)


def _proj_blocks(x, w, tm, rope_tabs=None):
    M, K = x.shape
    N = w.shape[1]
    in_specs = [pl.BlockSpec((tm, K), lambda i: (i, 0)), pl.BlockSpec((K, N), lambda i: (0, 0))]
    args = [x, w]
    body = _proj_blocks_kernel
    if rope_tabs is not None:
        n_pos = rope_tabs.shape[2] // tm
        in_specs.append(pl.BlockSpec((2, 3, tm, LANES), lambda i: (0, 0, i % n_pos, 0)))
        args.append(rope_tabs)
        body = _proj_rope_kernel
    return pl.pallas_call(
        body,
        grid=(M // tm,),
        in_specs=in_specs,
        out_specs=pl.BlockSpec((N // LANES, tm, LANES), lambda i: (0, i, 0)),
        out_shape=jax.ShapeDtypeStruct((N // LANES, M, LANES), BF16),
        compiler_params=_cp("parallel"),
        name="proj_rope" if rope_tabs is not None else "proj_blocks",
    )(*args)


ATT_TQ = 128
ATT_TK = ATT_TQ + 2 * N_SIDE
ATT_HALO = N_SIDE * DILATIONS[-1]
ATT_BLOCKS_PER_STEP = 8


def _att_kernel(q_ref, kp_ref, km_ref, kn_ref, vp_ref, vm_ref, vn_ref, o_ref,
                qf, kf, vf, acc, den, ma, mb, bias_ref, *, C, S):
    c = pl.program_id(2)
    H = ATT_HALO
    qf[...] = q_ref[...].astype(F32)
    kf[pl.ds(0, H), :] = kp_ref[...].astype(F32)
    kf[pl.ds(H, C), :] = km_ref[...].astype(F32)
    kf[pl.ds(H + C, H), :] = kn_ref[...].astype(F32)
    vf[pl.ds(0, H), :] = vp_ref[...].astype(F32)
    vf[pl.ds(H, C), :] = vm_ref[...].astype(F32)
    vf[pl.ds(H + C, H), :] = vn_ref[...].astype(F32)

    lane = lax.broadcasted_iota(jnp.int32, (ATT_TQ, LANES), 1)
    first_head = lane < ATT_HEAD_DIM
    qi = lax.broadcasted_iota(jnp.int32, (ATT_TQ, ATT_TK), 0)
    kj = lax.broadcasted_iota(jnp.int32, (ATT_TQ, ATT_TK), 1)
    band = jnp.abs(kj - qi - N_SIDE) <= N_SIDE
    ones = jnp.ones((ATT_TK, LANES), BF16)
    for case in range(4):
        ok = band
        if case & 1:
            ok = ok & (kj >= N_SIDE)
        if case & 2:
            ok = ok & (kj < ATT_TK - N_SIDE)
        bias_ref[case] = jnp.where(ok, 0.0, NEG_INF)

    for d in reversed(DILATIONS):
        n_blk = C // (d * ATT_TQ)
        seq_len = S // d
        init = d == DILATIONS[-1]

        def rows_of(t, d=d, n_blk=n_blk):
            r = t // n_blk
            ib = t % n_blk
            q0 = r + d * ib * ATT_TQ
            k0 = H + r + d * (ib * ATT_TQ - N_SIDE)
            if d == 1:
                return ib, pl.ds(q0, ATT_TQ), pl.ds(k0, ATT_TK)
            return ib, pl.ds(q0, ATT_TQ, stride=d), pl.ds(k0, ATT_TK, stride=d)

        def load(t, init=init):
            ib, rows_q, rows_k = rows_of(t)
            x = dict(ib=ib, q=qf[rows_q, :], k=kf[rows_k, :], v=vf[rows_k, :])
            if not init:
                x.update(ma=ma[rows_q, :], mb=mb[rows_q, :], acc=acc[rows_q, :], den=den[rows_q, :])
            return x

        def compute(x, d=d, seq_len=seq_len, init=init):
            kb = x["k"].astype(BF16)
            vb = jnp.concatenate([x["v"].astype(BF16), ones], axis=1)
            g0 = c * (C // d) + x["ib"] * ATT_TQ - N_SIDE
            case = (g0 < 0).astype(jnp.int32) + 2 * (g0 + ATT_TK > seq_len).astype(jnp.int32)
            bias = bias_ref[case]

            def head(q_h, m_old):
                s = lax.dot_general(q_h.astype(BF16), kb, (((1,), (1,)), ((), ())),
                                    preferred_element_type=F32)
                s = s + bias
                m_new = jnp.max(s, axis=-1, keepdims=True)
                if m_old is not None:
                    m_new = jnp.maximum(m_old, m_new)
                else:
                    m_new = jnp.broadcast_to(m_new, (ATT_TQ, LANES))
                p = jnp.exp2(s - jnp.concatenate([m_new, m_new], axis=1))
                pv = jnp.dot(p.astype(BF16), vb, preferred_element_type=F32)
                return pv, m_new

            pv_a, ma_new = head(jnp.where(first_head, x["q"], 0.0), x.get("ma"))
            pv_b, mb_new = head(jnp.where(first_head, 0.0, x["q"]), x.get("mb"))
            num = jnp.where(first_head, pv_a[:, :LANES], pv_b[:, :LANES])
            rsum = jnp.where(first_head, pv_a[:, LANES:], pv_b[:, LANES:])
            if init:
                return dict(ma=ma_new, mb=mb_new, acc=num, den=rsum)
            scale = jnp.where(first_head, jnp.exp2(x["ma"] - ma_new), jnp.exp2(x["mb"] - mb_new))
            return dict(ma=ma_new, mb=mb_new, acc=x["acc"] * scale + num, den=x["den"] * scale + rsum)

        def store(t, y):
            _, rows_q, _ = rows_of(t)
            ma[rows_q, :] = y["ma"]
            mb[rows_q, :] = y["mb"]
            acc[rows_q, :] = y["acc"]
            den[rows_q, :] = y["den"]

        def step(i, carry):
            ts = [i * ATT_BLOCKS_PER_STEP + u for u in range(ATT_BLOCKS_PER_STEP)]
            ys = [compute(x) for x in [load(t) for t in ts]]
            for t, y in zip(ts, ys):
                store(t, y)
            return carry

        lax.fori_loop(0, d * n_blk // ATT_BLOCKS_PER_STEP, step, 0)

    def finish(t, carry):
        r0 = pl.multiple_of(t * ATT_TQ, ATT_TQ)
        rows = pl.ds(r0, ATT_TQ)
        o_ref[rows, :] = (acc[rows, :] / den[rows, :]).astype(o_ref.dtype)
        return carry

    lax.fori_loop(0, C // ATT_TQ, finish, 0)


def _dilated_attention(qk, vm, B, S, C):
    H = ATT_HALO
    n_chunk = S // C
    nh = C // H
    n_halo = S // H

    def main(blk0):
        return pl.BlockSpec((None, C, LANES), lambda b, p, c: (blk0 + p, b * n_chunk + c, 0))

    def prev(blk0):
        return pl.BlockSpec((None, H, LANES),
                            lambda b, p, c: (blk0 + p, b * n_halo + jnp.maximum(c * nh - 1, 0), 0))

    def nxt(blk0):
        return pl.BlockSpec((None, H, LANES),
                            lambda b, p, c: (blk0 + p, b * n_halo + jnp.minimum((c + 1) * nh, n_halo - 1), 0))

    stat = pltpu.VMEM((C, LANES), F32)
    window = pltpu.VMEM((C + 2 * H, LANES), F32)
    return pl.pallas_call(
        functools.partial(_att_kernel, C=C, S=S),
        grid=(B, N_PAIRS, n_chunk),
        in_specs=[main(0), prev(N_PAIRS), main(N_PAIRS), nxt(N_PAIRS), prev(0), main(0), nxt(0)],
        out_specs=main(0),
        out_shape=jax.ShapeDtypeStruct((N_PAIRS, B * S, LANES), BF16),
        scratch_shapes=[stat, window, window, stat, stat, stat, stat,
                        pltpu.VMEM((4, ATT_TQ, ATT_TK), F32)],
        compiler_params=_cp("parallel", "parallel", "parallel"),
        name="dilated_attention",
    )(qk, qk, qk, qk, vm, vm, vm)


def _mem_att_kernel(q_ref, k_ref, v_ref, o_ref):
    s = lax.dot_general(q_ref[...], k_ref[...], (((1,), (1,)), ((), ())),
                        preferred_element_type=F32) * (MEM_HEAD_DIM ** -0.5)
    p = jnp.exp(s - jnp.max(s, axis=-1, keepdims=True))
    vb = jnp.concatenate([v_ref[...], jnp.ones((N_MEM_TOKENS, LANES), BF16)], axis=1)
    pv = jnp.dot(p.astype(BF16), vb, preferred_element_type=F32)
    o_ref[...] = (pv[:, :LANES] / pv[:, LANES:]).astype(o_ref.dtype)


def _memory_attention(vm, kv, B, S, ts):
    n_t = S // ts
    return pl.pallas_call(
        _mem_att_kernel,
        grid=(B, N_MEM_HEADS, n_t),
        in_specs=[pl.BlockSpec((None, ts, LANES), lambda b, hd, i: (N_PAIRS + hd, b * n_t + i, 0)),
                  pl.BlockSpec((None, N_MEM_TOKENS, LANES), lambda b, hd, i: (b, 0, hd)),
                  pl.BlockSpec((None, N_MEM_TOKENS, LANES), lambda b, hd, i: (b, 0, N_MEM_HEADS + hd))],
        out_specs=pl.BlockSpec((None, ts, LANES), lambda b, hd, i: (hd, b * n_t + i, 0)),
        out_shape=jax.ShapeDtypeStruct((N_MEM_HEADS, B * S, LANES), BF16),
        compiler_params=_cp("parallel", "parallel", "parallel"),
        name="memory_attention",
    )(vm, kv, kv)


def _out_proj_kernel(c_ref, a_ref, m_ref, x_ref, w_ref, g_ref, b_ref, of_ref, ob_ref):
    half = x_ref.shape[0] // 2
    for r0 in (0, half):
        rows = pl.ds(r0, half)
        mixed = jnp.concatenate([c_ref[rows, :]] + [a_ref[j, rows, :] for j in range(N_PAIRS)]
                                + [m_ref[j, rows, :] for j in range(N_MEM_HEADS)], axis=1)
        y = jnp.dot(mixed, w_ref[...], preferred_element_type=F32)
        z = _layernorm_rows(ALPHA * x_ref[rows, :] + y, g_ref[...], b_ref[...])
        of_ref[rows, :] = z
        ob_ref[rows, :] = z.astype(BF16)


def _out_proj_ln(conv_o, att_o, mem_o, x, w_out, g, b, tm):
    T = x.shape[0]
    row = lambda w: pl.BlockSpec((tm, w), lambda i: (i, 0))
    slabs = lambda n: pl.BlockSpec((n, tm, LANES), lambda i: (0, i, 0))
    vec = pl.BlockSpec((1, D_MODEL), lambda i: (0, 0))
    return pl.pallas_call(
        _out_proj_kernel,
        grid=(T // tm,),
        in_specs=[row(D_CONV), slabs(N_PAIRS), slabs(N_MEM_HEADS), row(D_MODEL),
                  pl.BlockSpec((D_MODEL, D_MODEL), lambda i: (0, 0), pipeline_mode=pl.Buffered(1)),
                  vec, vec],
        out_specs=[row(D_MODEL), row(D_MODEL)],
        out_shape=[jax.ShapeDtypeStruct((T, D_MODEL), F32), jax.ShapeDtypeStruct((T, D_MODEL), BF16)],
        compiler_params=_cp("parallel"),
        name="out_proj_ln",
    )(conv_o, att_o, mem_o, x, w_out, g.reshape(1, -1), b.reshape(1, -1))


def _swiglu_step(xb, w1_ref, w3_ref, w2_ref):
    h1 = jnp.dot(xb, w1_ref[...], preferred_element_type=F32)
    h3 = jnp.dot(xb, w3_ref[...], preferred_element_type=F32)
    act = (h1 * jax.nn.sigmoid(h1) * h3).astype(BF16)
    return jnp.dot(act, w2_ref[...], preferred_element_type=F32)


def _ffn_kernel(xb_ref, w1_ref, w3_ref, w2_ref, x_ref, g_ref, b_ref, of_ref, ob_ref, acc):
    f = pl.program_id(1)

    @pl.when(f == 0)
    def _():
        acc[...] = jnp.zeros_like(acc)

    acc[...] += _swiglu_step(xb_ref[...], w1_ref, w3_ref, w2_ref)

    @pl.when(f == pl.num_programs(1) - 1)
    def _():
        z = _layernorm_rows(ALPHA * x_ref[...] + acc[...], g_ref[...], b_ref[...])
        of_ref[...] = z
        ob_ref[...] = z.astype(BF16)


def _ffn_ln(xb, x, w1, w3, w2, g, b, tm, tf):
    T = x.shape[0]
    F = w1.shape[1]
    row = pl.BlockSpec((tm, D_MODEL), lambda i, f: (i, 0))
    vec = pl.BlockSpec((1, D_MODEL), lambda i, f: (0, 0))
    return pl.pallas_call(
        _ffn_kernel,
        grid=(T // tm, F // tf),
        in_specs=[row,
                  pl.BlockSpec((D_MODEL, tf), lambda i, f: (0, f)),
                  pl.BlockSpec((D_MODEL, tf), lambda i, f: (0, f)),
                  pl.BlockSpec((tf, D_MODEL), lambda i, f: (f, 0)),
                  row, vec, vec],
        out_specs=[row, row],
        out_shape=[jax.ShapeDtypeStruct((T, D_MODEL), F32), jax.ShapeDtypeStruct((T, D_MODEL), BF16)],
        scratch_shapes=[pltpu.VMEM((tm, D_MODEL), F32)],
        compiler_params=_cp("parallel", "arbitrary"),
        name="dense_ffn_ln",
    )(xb, w1, w3, w2, x, g.reshape(1, -1), b.reshape(1, -1))


def _split_bf16(x):
    hi = x.astype(BF16)
    return hi, (x - hi.astype(F32)).astype(BF16)


def _router_kernel(x_ref, r_ref, idx_ref, gate_ref):
    xh, xl = _split_bf16(x_ref[...])
    rh, rl = _split_bf16(r_ref[...])
    dot = functools.partial(jnp.dot, preferred_element_type=F32)
    logits = dot(xh, rh) + (dot(xl, rh) + dot(xh, rl))
    e = lax.broadcasted_iota(jnp.int32, logits.shape, 1)
    v1 = jnp.max(logits, axis=-1, keepdims=True)
    i1 = jnp.min(jnp.where(logits == v1, e, N_EXPERTS), axis=-1, keepdims=True)
    rest = jnp.where(e == i1, -jnp.inf, logits)
    v2 = jnp.max(rest, axis=-1, keepdims=True)
    i2 = jnp.min(jnp.where(rest == v2, e, N_EXPERTS), axis=-1, keepdims=True)
    e2 = jnp.exp(v2 - v1)
    denom = 1.0 + e2
    idx_ref[...] = jnp.concatenate([i1, i2], axis=1)
    gate_ref[...] = jnp.concatenate([1.0 / denom, e2 / denom], axis=1)


def _router(x, router, tm):
    T = x.shape[0]
    return pl.pallas_call(
        _router_kernel,
        grid=(T // tm,),
        in_specs=[pl.BlockSpec((tm, D_MODEL), lambda i: (i, 0)),
                  pl.BlockSpec((D_MODEL, N_EXPERTS), lambda i: (0, 0))],
        out_specs=[pl.BlockSpec((tm, 2), lambda i: (i, 0)), pl.BlockSpec((tm, 2), lambda i: (i, 0))],
        out_shape=[jax.ShapeDtypeStruct((T, 2), jnp.int32), jax.ShapeDtypeStruct((T, 2), F32)],
        compiler_params=_cp("parallel"),
        name="router_top2",
    )(x, router)


def _routing_plan(idx, tm):
    T = idx.shape[0]
    i32 = jnp.int32
    n_visits = 2 * T // tm + N_EXPERTS - 1
    flat = idx.reshape(-1)
    onehot = (flat[:, None] == jnp.arange(N_EXPERTS, dtype=i32)[None, :]).astype(i32)
    rank = jnp.cumsum(onehot, axis=0) - onehot
    counts = jnp.sum(onehot, axis=0)
    ends = jnp.cumsum(counts)
    starts = ends - counts
    pos = jnp.sum((rank + starts[None, :]) * onehot, axis=1).astype(i32)
    first_tile = starts // tm
    n_e = jnp.where(counts > 0, (ends - 1) // tm - first_tile + 1, 0)
    v_end = jnp.cumsum(n_e)
    v_start = v_end - n_e
    v = jnp.arange(n_visits, dtype=i32)
    valid = v < v_end[-1]
    vc = jnp.minimum(v, v_end[-1] - 1)
    e = jnp.minimum(jnp.sum(vc[:, None] >= v_end[None, :], axis=1), N_EXPERTS - 1).astype(i32)
    tile = first_tile[e] + (vc - v_start[e])
    lo = jnp.maximum(starts[e], tile * tm) - tile * tm
    hi = jnp.minimum(ends[e], (tile + 1) * tm) - tile * tm
    first = jnp.concatenate([jnp.ones((1,), bool), tile[1:] != tile[:-1]])
    table = [tile, e, lo, jnp.where(valid, hi, lo), first & valid, valid]
    return pos, [a.astype(i32) for a in table]


def _dispatch_kernel(pos_ref, x_ref, o_hbm, sem, *, td):
    def copy(j, k):
        return pltpu.make_async_copy(x_ref.at[pl.ds(j, 1)],
                                     o_hbm.at[pl.ds(pos_ref[0, 2 * j + k], 1)], sem)

    def start(j, carry):
        copy(j, 0).start()
        copy(j, 1).start()
        return carry

    def wait(j, carry):
        copy(j, 0).wait()
        copy(j, 1).wait()
        return carry

    lax.fori_loop(0, td, start, 0, unroll=8)
    lax.fori_loop(0, td, wait, 0, unroll=8)


def _dispatch(x, pos, td):
    T = x.shape[0]
    return pl.pallas_call(
        functools.partial(_dispatch_kernel, td=td),
        grid=(T // td,),
        in_specs=[pl.BlockSpec((None, 1, 2 * td), lambda i: (i, 0, 0), memory_space=pltpu.SMEM),
                  pl.BlockSpec((td, D_MODEL), lambda i: (i, 0))],
        out_specs=pl.BlockSpec(memory_space=pl.ANY),
        out_shape=jax.ShapeDtypeStruct((2 * T, D_MODEL), F32),
        scratch_shapes=[pltpu.SemaphoreType.DMA],
        compiler_params=_cp("arbitrary"),
        name="expert_dispatch",
    )(pos.reshape(T // td, 1, 2 * td), x)


def _moe_ffn_kernel(tile_ref, exp_ref, lo_ref, hi_ref, first_ref, valid_ref,
                    x_ref, w1_ref, w3_ref, w2_ref, o_ref, xb, acc):
    del tile_ref, exp_ref
    v = pl.program_id(0)
    f = pl.program_id(1)
    valid = valid_ref[v] > 0

    @pl.when(valid & (f == 0))
    def _():
        row = lax.broadcasted_iota(jnp.int32, x_ref.shape, 0)
        mine = (row >= lo_ref[v]) & (row < hi_ref[v])
        xb[...] = jnp.where(mine, x_ref[...], 0.0).astype(BF16)

    @pl.when((first_ref[v] > 0) & (f == 0))
    def _():
        acc[...] = jnp.zeros_like(acc)

    @pl.when(valid)
    def _():
        acc[...] += _swiglu_step(xb[...], w1_ref, w3_ref, w2_ref)

    @pl.when(valid & (f == pl.num_programs(1) - 1))
    def _():
        o_ref[...] = acc[...]


def _moe_ffn(xs, table, w1, w3, w2, tm, tf):
    n_rows = xs.shape[0]
    n_visits = table[0].shape[0]
    nf = w1.shape[2] // tf

    def fidx(v, f, valid):
        return jnp.where(valid[v] > 0, f, nf - 1)

    row = pl.BlockSpec((tm, D_MODEL), lambda v, f, tile, *_: (tile[v], 0))
    up = pl.BlockSpec((None, D_MODEL, tf),
                      lambda v, f, tile, e, lo, hi, first, valid: (e[v], 0, fidx(v, f, valid)))
    down = pl.BlockSpec((None, tf, D_MODEL),
                        lambda v, f, tile, e, lo, hi, first, valid: (e[v], fidx(v, f, valid), 0))
    grid_spec = pltpu.PrefetchScalarGridSpec(
        num_scalar_prefetch=len(table),
        grid=(n_visits, nf),
        in_specs=[row, up, up, down],
        out_specs=row,
        scratch_shapes=[pltpu.VMEM((tm, D_MODEL), BF16), pltpu.VMEM((tm, D_MODEL), F32)],
    )
    return pl.pallas_call(
        _moe_ffn_kernel,
        grid_spec=grid_spec,
        out_shape=jax.ShapeDtypeStruct((n_rows, D_MODEL), F32),
        compiler_params=_cp("arbitrary", "arbitrary"),
        name="expert_ffn",
    )(*table, xs, w1, w3, w2)


def _combine_kernel(pos_ref, pos_next_ref, gate_ref, x_ref, g_ref, b_ref, y_hbm, of_ref, ob_ref,
                    ybuf, sem, *, tc):
    i = pl.program_id(0)
    slot = i % 2

    def copy(p_ref, s, j, k):
        return pltpu.make_async_copy(y_hbm.at[pl.ds(p_ref[0, 2 * j + k], 1)],
                                     ybuf.at[s, k, pl.ds(j, 1)], sem.at[s])

    def start_all(p_ref, s):
        def body(j, carry):
            copy(p_ref, s, j, 0).start()
            copy(p_ref, s, j, 1).start()
            return carry
        lax.fori_loop(0, tc, body, 0, unroll=8)

    def wait_all(p_ref, s):
        def body(j, carry):
            copy(p_ref, s, j, 0).wait()
            copy(p_ref, s, j, 1).wait()
            return carry
        lax.fori_loop(0, tc, body, 0, unroll=8)

    @pl.when(i == 0)
    def _():
        start_all(pos_ref, 0)

    @pl.when(i + 1 < pl.num_programs(0))
    def _():
        start_all(pos_next_ref, 1 - slot)

    wait_all(pos_ref, slot)
    gate = gate_ref[...]
    f = gate[:, 0:1] * ybuf[slot, 0] + gate[:, 1:2] * ybuf[slot, 1]
    z = _layernorm_rows(ALPHA * x_ref[...] + f, g_ref[...], b_ref[...])
    of_ref[...] = z
    ob_ref[...] = z.astype(BF16)


def _combine_ln(ys, pos, gates, x, g, b, tc):
    T = x.shape[0]
    n = T // tc
    row = pl.BlockSpec((tc, D_MODEL), lambda i: (i, 0))
    vec = pl.BlockSpec((1, D_MODEL), lambda i: (0, 0))
    pos3 = pos.reshape(n, 1, 2 * tc)
    return pl.pallas_call(
        functools.partial(_combine_kernel, tc=tc),
        grid=(n,),
        in_specs=[pl.BlockSpec((None, 1, 2 * tc), lambda i: (i, 0, 0), memory_space=pltpu.SMEM),
                  pl.BlockSpec((None, 1, 2 * tc), lambda i: (jnp.minimum(i + 1, n - 1), 0, 0),
                               memory_space=pltpu.SMEM),
                  pl.BlockSpec((tc, 2), lambda i: (i, 0)),
                  row, vec, vec,
                  pl.BlockSpec(memory_space=pl.ANY)],
        out_specs=[row, row],
        out_shape=[jax.ShapeDtypeStruct((T, D_MODEL), F32), jax.ShapeDtypeStruct((T, D_MODEL), BF16)],
        scratch_shapes=[pltpu.VMEM((2, 2, tc, D_MODEL), F32), pltpu.SemaphoreType.DMA((2,))],
        compiler_params=_cp("arbitrary"),
        name="expert_combine_ln",
    )(pos3, pos3, gates, x, g.reshape(1, -1), b.reshape(1, -1), ys)


def _tiles(B, S):
    T = B * S
    return dict(
        tm_proj=min(512, T), tm_kv=min(512, B * N_MEM_TOKENS),
        ts_conv=min(256, S), att_chunk=min(4096, S), ts_mem=min(1024, S),
        tm_out=min(512, T), tm_ffn=min(512, T), tf=512, tf_moe=1024,
        tm_route=min(512, T), td=min(256, T), tm_moe=min(512, T), tc=min(256, T),
    )


def _trunk(x, mem, p, wb):
    B, S, _ = x.shape
    T = B * S
    t = _tiles(B, S)
    if S % (DILATIONS[-1] * ATT_TQ) or S % ATT_HALO:
        raise ValueError("sequence length must be a multiple of 2048")
    rope_tabs = _rope_tables(S)
    x = x.reshape(T, D_MODEL)
    xb = x.astype(BF16)
    mem_b = mem.reshape(B * N_MEM_TOKENS, D_MODEL).astype(BF16)
    for l in range(DEPTH):
        w_in = wb["w_in"][l]
        ag = _matmul(xb, w_in[:, :2 * D_CONV], t["tm_proj"], 2 * D_CONV, BF16)
        qk = _proj_blocks(xb, w_in[:, 2 * D_CONV:2 * D_CONV + 2 * D_ATT], t["tm_proj"], rope_tabs)
        vm = _proj_blocks(xb, w_in[:, 2 * D_CONV + 2 * D_ATT:], t["tm_proj"])
        kv = _matmul(mem_b, wb["w_mem_kv"][l], t["tm_kv"], 2 * D_MEM, BF16)
        kv = kv.reshape(B, N_MEM_TOKENS, 2 * D_MEM)
        conv_o = _conformer_conv(ag.reshape(B, S, 2 * D_CONV), p["conv_w"][l], p["conv_b"][l],
                                 p["conv_ln_g"][l], p["conv_ln_b"][l], t["ts_conv"])
        att_o = _dilated_attention(qk, vm, B, S, t["att_chunk"])
        mem_o = _memory_attention(vm, kv, B, S, t["ts_mem"])
        x, xb = _out_proj_ln(conv_o.reshape(T, D_CONV), att_o, mem_o, x, wb["w_out"][l],
                             p["ln1_g"][l], p["ln1_b"][l], t["tm_out"])
        i = l // 2
        if l % 2 == 0:
            x, xb = _ffn_ln(xb, x, wb["ffn_w1"][i], wb["ffn_w3"][i], wb["ffn_w2"][i],
                            p["ln2_g"][l], p["ln2_b"][l], t["tm_ffn"], t["tf"])
        else:
            idx, gates = _router(x, p["moe_router"][i], t["tm_route"])
            pos, table = _routing_plan(idx, t["tm_moe"])
            xs = _dispatch(x, pos, t["td"])
            ys = _moe_ffn(xs, table, wb["moe_w1"][i], wb["moe_w3"][i], wb["moe_w2"][i],
                          t["tm_moe"], t["tf_moe"])
            x, xb = _combine_ln(ys, pos, gates, x, p["ln2_g"][l], p["ln2_b"][l], t["tc"])
    return x.reshape(B, S, D_MODEL)


def kernel(x_prompt, x_sample, mem_prompt, mem_sample, w_in, conv_w, conv_b, conv_ln_g, conv_ln_b,
           w_mem_kv, w_out, ln1_g, ln1_b, ffn_w1, ffn_w3, ffn_w2, moe_router, moe_w1, moe_w3, moe_w2,
           ln2_g, ln2_b):
    p = dict(conv_w=conv_w, conv_b=conv_b, conv_ln_g=conv_ln_g, conv_ln_b=conv_ln_b,
             ln1_g=ln1_g, ln1_b=ln1_b, ln2_g=ln2_g, ln2_b=ln2_b, moe_router=moe_router)
    wb = dict(w_in=w_in, w_mem_kv=w_mem_kv, w_out=w_out, ffn_w1=ffn_w1, ffn_w3=ffn_w3, ffn_w2=ffn_w2,
              moe_w1=moe_w1, moe_w3=moe_w3, moe_w2=moe_w2)
    wb = {k: v.astype(BF16) for k, v in wb.items()}
    y_prompt = _trunk(x_prompt, mem_prompt, p, wb)
    y_sample = _trunk(x_sample, mem_sample, p, wb)
    return (y_prompt, y_sample)
```

```python
import functools
import math

import jax
import jax.numpy as jnp
import numpy as np
from jax import lax
from jax.experimental import pallas as pl
from jax.experimental.pallas import tpu as pltpu

F32 = jnp.float32
BF16 = jnp.bfloat16

D_MODEL = 2048
DEPTH = 2
D_CONV = 768
D_ATT = 768
D_MEM = 512
ATT_HEAD_DIM = 64
N_MEM_HEADS = 4
MEM_HEAD_DIM = 128
N_MEM_TOKENS = 256
D_IN = 2 * D_CONV + 3 * D_ATT + D_MEM
CONV_WIDTH = 31
CONV_PAD = CONV_WIDTH // 2
DILATIONS = (1, 4, 16)
N_SIDE = 64
ROT_DIM = 16
ROPE_THETA = 500000.0
N_EXPERTS = 8
ALPHA = (2 * DEPTH) ** 0.25
LN_EPS = 1e-5
NEG_INF = -1e30

LANES = 128
SUBLANES = 8
N_PAIRS = D_ATT // LANES

VMEM_LIMIT = 56 * 1024 * 1024


def _cp(*sem):
    return pltpu.CompilerParams(dimension_semantics=sem, vmem_limit_bytes=VMEM_LIMIT)


def _layernorm_rows(z, g, b):
    mu = jnp.mean(z, axis=-1, keepdims=True)
    zc = z - mu
    var = jnp.mean(zc * zc, axis=-1, keepdims=True)
    return zc * lax.rsqrt(var + LN_EPS) * g + b


def _matmul_kernel(x_ref, w_ref, o_ref):
    o_ref[...] = jnp.dot(x_ref[...].astype(BF16), w_ref[...],
                         preferred_element_type=F32).astype(o_ref.dtype)


def _matmul(x, w, tm, tn, out_dtype):
    M, K = x.shape
    N = w.shape[1]
    return pl.pallas_call(
        _matmul_kernel,
        grid=(N // tn, M // tm),
        in_specs=[pl.BlockSpec((tm, K), lambda j, i: (i, 0)),
                  pl.BlockSpec((K, tn), lambda j, i: (0, j))],
        out_specs=pl.BlockSpec((tm, tn), lambda j, i: (i, j)),
        out_shape=jax.ShapeDtypeStruct((M, N), out_dtype),
        compiler_params=_cp("parallel", "parallel"),
        name="proj_matmul",
    )(x, w)


CONV_HALO = 16
CONV_ROWS = 32


def _conv_kernel(ap_ref, am_ref, an_ref, gp_ref, gm_ref, gn_ref, w_ref, cb_ref, lg_ref, lb_ref,
                 o_ref, u_ref, ush_ref, y_ref, *, ts):
    i = pl.program_id(1)
    last = pl.num_programs(1) - 1

    def glu(a_ref, g_ref):
        return a_ref[...].astype(F32) * jax.nn.sigmoid(g_ref[...].astype(F32))

    u_ref[pl.ds(CONV_HALO, ts), :] = glu(am_ref, gm_ref)
    u_ref[pl.ds(0, CONV_HALO), :] = jnp.where(i > 0, glu(ap_ref, gp_ref), 0.0)
    u_ref[pl.ds(CONV_HALO + ts, CONV_HALO), :] = jnp.where(i < last, glu(an_ref, gn_ref), 0.0)

    cb = cb_ref[...]
    lg = lg_ref[...]
    lb = lb_ref[...]

    def fill(c, carry):
        r = pl.multiple_of(c * SUBLANES, SUBLANES)
        two = u_ref[pl.ds(r, 2 * SUBLANES), :]
        for m in range(SUBLANES):
            ush_ref[m, pl.ds(r, SUBLANES), :] = two[m:m + SUBLANES]
        return carry

    lax.fori_loop(0, (ts + 2 * CONV_HALO) // SUBLANES - 1, fill, 0)

    n_sub = CONV_ROWS // SUBLANES

    def conv(c, carry):
        r0 = pl.multiple_of(c * CONV_ROWS, CONV_ROWS)
        acc = [jnp.zeros((SUBLANES, D_CONV), F32) for _ in range(n_sub)]
        for t in range(CONV_WIDTH):
            q, m = divmod(t + CONV_HALO - CONV_PAD, SUBLANES)
            w_t = w_ref[t]
            for k in range(n_sub):
                acc[k] = acc[k] + w_t * ush_ref[m, pl.ds(r0 + (q + k) * SUBLANES, SUBLANES), :]
        for k in range(n_sub):
            y_ref[pl.ds(r0 + k * SUBLANES, SUBLANES), :] = acc[k] + cb
        return carry

    lax.fori_loop(0, ts // CONV_ROWS, conv, 0)

    def norm(c, carry):
        r0 = pl.multiple_of(c * CONV_HALO, CONV_HALO)
        y = _layernorm_rows(y_ref[pl.ds(r0, CONV_HALO), :], lg, lb)
        o_ref[pl.ds(r0, CONV_HALO), :] = (y * jax.nn.sigmoid(y)).astype(o_ref.dtype)
        return carry

    lax.fori_loop(0, ts // CONV_HALO, norm, 0, unroll=8)


def _conformer_conv(h, conv_w, conv_b, ln_g, ln_b, ts):
    B, S, _ = h.shape
    nh = ts // CONV_HALO
    n_halo = S // CONV_HALO
    w_b = jnp.broadcast_to(conv_w[:, None, :], (CONV_WIDTH, SUBLANES, D_CONV)).astype(F32)

    def main(col):
        return pl.BlockSpec((None, ts, D_CONV), lambda b, i: (b, i, col))

    def prev(col):
        return pl.BlockSpec((None, CONV_HALO, D_CONV),
                            lambda b, i: (b, jnp.maximum(i * nh - 1, 0), col))

    def nxt(col):
        return pl.BlockSpec((None, CONV_HALO, D_CONV),
                            lambda b, i: (b, jnp.minimum((i + 1) * nh, n_halo - 1), col))

    vec = pl.BlockSpec((1, D_CONV), lambda b, i: (0, 0))
    return pl.pallas_call(
        functools.partial(_conv_kernel, ts=ts),
        grid=(B, S // ts),
        in_specs=[prev(0), main(0), nxt(0), prev(1), main(1), nxt(1),
                  pl.BlockSpec((CONV_WIDTH, SUBLANES, D_CONV), lambda b, i: (0, 0, 0)),
                  vec, vec, vec],
        out_specs=pl.BlockSpec((None, ts, D_CONV), lambda b, i: (b, i, 0)),
        out_shape=jax.ShapeDtypeStruct((B, S, D_CONV), BF16),
        scratch_shapes=[pltpu.VMEM((ts + 2 * CONV_HALO, D_CONV), F32),
                        pltpu.VMEM((SUBLANES, ts + 2 * CONV_HALO - SUBLANES, D_CONV), F32),
                        pltpu.VMEM((ts, D_CONV), F32)],
        compiler_params=_cp("parallel", "parallel"),
        name="conformer_conv",
    )(h, h, h, h, h, h, w_b, conv_b.reshape(1, -1), ln_g.reshape(1, -1), ln_b.reshape(1, -1))


def _rope_tables(S):
    half = ROT_DIM // 2
    inv_freq = jnp.power(ROPE_THETA, -jnp.arange(0, ROT_DIM, 2, dtype=F32) / ROT_DIM)
    ang = jnp.arange(S, dtype=F32)[:, None] * inv_freq[None, :]
    cos, sin = jnp.cos(ang), jnp.sin(ang)
    zeros = jnp.zeros((S, ATT_HEAD_DIM - ROT_DIM), F32)
    z8 = jnp.zeros((S, half), F32)
    c_head = jnp.concatenate([cos, cos, zeros + 1.0], axis=1)
    up_head = jnp.concatenate([-sin, z8, zeros], axis=1)
    dn_head = jnp.concatenate([z8, sin, zeros], axis=1)
    tabs = jnp.stack([jnp.tile(t, (1, LANES // ATT_HEAD_DIM)) for t in (c_head, up_head, dn_head)])
    q_scale = ATT_HEAD_DIM ** -0.5 * math.log2(math.e)
    return jnp.stack([tabs * q_scale, tabs])


def _in_proj_kernel(x_ref, w_ref, t_ref, ag_ref, qk_ref, vm_ref):
    xb = x_ref[...].astype(BF16)
    n_ag, n_qk = 2 * D_CONV, 2 * D_ATT
    ag_ref[...] = jnp.dot(xb, w_ref[:, pl.ds(0, n_ag)], preferred_element_type=F32).astype(BF16)
    y = jnp.dot(xb, w_ref[:, pl.ds(n_ag, n_qk)], preferred_element_type=F32)
    for j in range(qk_ref.shape[0]):
        x = y[:, j * LANES:(j + 1) * LANES]
        t = t_ref[j // N_PAIRS]
        up = pltpu.roll(x, LANES - ROT_DIM // 2, 1)
        dn = pltpu.roll(x, ROT_DIM // 2, 1)
        qk_ref[j] = (x * t[0] + up * t[1] + dn * t[2]).astype(BF16)
    y = jnp.dot(xb, w_ref[:, pl.ds(n_ag + n_qk, D_ATT + D_MEM)], preferred_element_type=F32)
    for j in range(vm_ref.shape[0]):
        vm_ref[j] = y[:, j * LANES:(j + 1) * LANES].astype(BF16)


def _in_proj(x, w, rope_tabs, tm):
    M, K = x.shape
    n_pos = rope_tabs.shape[2] // tm
    n_qk = 2 * D_ATT // LANES
    n_vm = (D_ATT + D_MEM) // LANES
    slabs = lambda n: pl.BlockSpec((n, tm, LANES), lambda i: (0, i, 0))
    return pl.pallas_call(
        _in_proj_kernel,
        grid=(M // tm,),
        in_specs=[pl.BlockSpec((tm, K), lambda i: (i, 0)),
                  pl.BlockSpec((K, D_IN), lambda i: (0, 0)),
                  pl.BlockSpec((2, 3, tm, LANES), lambda i: (0, 0, i % n_pos, 0))],
        out_specs=[pl.BlockSpec((tm, 2 * D_CONV), lambda i: (i, 0)), slabs(n_qk), slabs(n_vm)],
        out_shape=[jax.ShapeDtypeStruct((M, 2 * D_CONV), BF16),
                   jax.ShapeDtypeStruct((n_qk, M, LANES), BF16),
                   jax.ShapeDtypeStruct((n_vm, M, LANES), BF16)],
        compiler_params=_cp("parallel"),
        name="in_proj",
    )(x, w, rope_tabs)


ATT_TQ = 128
ATT_TK = ATT_TQ + 2 * N_SIDE
ATT_HALO = N_SIDE * DILATIONS[-1]
ATT_BLOCKS_PER_STEP = 8


def _att_kernel(q_ref, kp_ref, km_ref, kn_ref, vp_ref, vm_ref, vn_ref, o_ref,
                qf, kf, vf, acc, den, ma, mb, bias_ref, *, C, S):
    c = pl.program_id(2)
    H = ATT_HALO
    qf[...] = q_ref[...].astype(F32)
    kf[pl.ds(0, H), :] = kp_ref[...].astype(F32)
    kf[pl.ds(H, C), :] = km_ref[...].astype(F32)
    kf[pl.ds(H + C, H), :] = kn_ref[...].astype(F32)
    vf[pl.ds(0, H), :] = vp_ref[...].astype(F32)
    vf[pl.ds(H, C), :] = vm_ref[...].astype(F32)
    vf[pl.ds(H + C, H), :] = vn_ref[...].astype(F32)

    lane = lax.broadcasted_iota(jnp.int32, (ATT_TQ, LANES), 1)
    first_head = lane < ATT_HEAD_DIM
    qi = lax.broadcasted_iota(jnp.int32, (ATT_TQ, ATT_TK), 0)
    kj = lax.broadcasted_iota(jnp.int32, (ATT_TQ, ATT_TK), 1)
    band = jnp.abs(kj - qi - N_SIDE) <= N_SIDE
    ones = jnp.ones((ATT_TK, LANES), BF16)
    for case in range(4):
        ok = band
        if case & 1:
            ok = ok & (kj >= N_SIDE)
        if case & 2:
            ok = ok & (kj < ATT_TK - N_SIDE)
        bias_ref[case] = jnp.where(ok, 0.0, NEG_INF)

    for d in reversed(DILATIONS):
        n_blk = C // (d * ATT_TQ)
        seq_len = S // d
        init = d == DILATIONS[-1]

        def rows_of(t, d=d, n_blk=n_blk):
            r = t // n_blk
            ib = t % n_blk
            q0 = r + d * ib * ATT_TQ
            k0 = H + r + d * (ib * ATT_TQ - N_SIDE)
            if d == 1:
                return ib, pl.ds(q0, ATT_TQ), pl.ds(k0, ATT_TK)
            return ib, pl.ds(q0, ATT_TQ, stride=d), pl.ds(k0, ATT_TK, stride=d)

        def load(t, init=init):
            ib, rows_q, rows_k = rows_of(t)
            x = dict(ib=ib, q=qf[rows_q, :], k=kf[rows_k, :], v=vf[rows_k, :])
            if not init:
                x.update(ma=ma[rows_q, :], mb=mb[rows_q, :], acc=acc[rows_q, :], den=den[rows_q, :])
            return x

        def compute(x, d=d, seq_len=seq_len, init=init):
            kb = x["k"].astype(BF16)
            vb = jnp.concatenate([x["v"].astype(BF16), ones], axis=1)
            g0 = c * (C // d) + x["ib"] * ATT_TQ - N_SIDE
            case = (g0 < 0).astype(jnp.int32) + 2 * (g0 + ATT_TK > seq_len).astype(jnp.int32)
            bias = bias_ref[case]

            def head(q_h, m_old):
                s = lax.dot_general(q_h.astype(BF16), kb, (((1,), (1,)), ((), ())),
                                    preferred_element_type=F32)
                s = s + bias
                m_new = jnp.max(s, axis=-1, keepdims=True)
                if m_old is not None:
                    m_new = jnp.maximum(m_old, m_new)
                else:
                    m_new = jnp.broadcast_to(m_new, (ATT_TQ, LANES))
                p = jnp.exp2(s - jnp.concatenate([m_new, m_new], axis=1))
                pv = jnp.dot(p.astype(BF16), vb, preferred_element_type=F32)
                return pv, m_new

            pv_a, ma_new = head(jnp.where(first_head, x["q"], 0.0), x.get("ma"))
            pv_b, mb_new = head(jnp.where(first_head, 0.0, x["q"]), x.get("mb"))
            num = jnp.where(first_head, pv_a[:, :LANES], pv_b[:, :LANES])
            rsum = jnp.where(first_head, pv_a[:, LANES:], pv_b[:, LANES:])
            if init:
                return dict(ma=ma_new, mb=mb_new, acc=num, den=rsum)
            scale = jnp.where(first_head, jnp.exp2(x["ma"] - ma_new), jnp.exp2(x["mb"] - mb_new))
            return dict(ma=ma_new, mb=mb_new, acc=x["acc"] * scale + num, den=x["den"] * scale + rsum)

        def store(t, y):
            _, rows_q, _ = rows_of(t)
            ma[rows_q, :] = y["ma"]
            mb[rows_q, :] = y["mb"]
            acc[rows_q, :] = y["acc"]
            den[rows_q, :] = y["den"]

        def step(i, carry):
            ts = [i * ATT_BLOCKS_PER_STEP + u for u in range(ATT_BLOCKS_PER_STEP)]
            ys = [compute(x) for x in [load(t) for t in ts]]
            for t, y in zip(ts, ys):
                store(t, y)
            return carry

        lax.fori_loop(0, d * n_blk // ATT_BLOCKS_PER_STEP, step, 0)

    def finish(t, carry):
        r0 = pl.multiple_of(t * ATT_TQ, ATT_TQ)
        rows = pl.ds(r0, ATT_TQ)
        o_ref[rows, :] = (acc[rows, :] / den[rows, :]).astype(o_ref.dtype)
        return carry

    lax.fori_loop(0, C // ATT_TQ, finish, 0)


def _dilated_attention(qk, vm, B, S, C):
    H = ATT_HALO
    n_chunk = S // C
    nh = C // H
    n_halo = S // H

    def main(blk0):
        return pl.BlockSpec((None, C, LANES), lambda b, p, c: (blk0 + p, b * n_chunk + c, 0))

    def prev(blk0):
        return pl.BlockSpec((None, H, LANES),
                            lambda b, p, c: (blk0 + p, b * n_halo + jnp.maximum(c * nh - 1, 0), 0))

    def nxt(blk0):
        return pl.BlockSpec((None, H, LANES),
                            lambda b, p, c: (blk0 + p, b * n_halo + jnp.minimum((c + 1) * nh, n_halo - 1), 0))

    stat = pltpu.VMEM((C, LANES), F32)
    window = pltpu.VMEM((C + 2 * H, LANES), F32)
    return pl.pallas_call(
        functools.partial(_att_kernel, C=C, S=S),
        grid=(B, N_PAIRS, n_chunk),
        in_specs=[main(0), prev(N_PAIRS), main(N_PAIRS), nxt(N_PAIRS), prev(0), main(0), nxt(0)],
        out_specs=main(0),
        out_shape=jax.ShapeDtypeStruct((N_PAIRS, B * S, LANES), BF16),
        scratch_shapes=[stat, window, window, stat, stat, stat, stat,
                        pltpu.VMEM((4, ATT_TQ, ATT_TK), F32)],
        compiler_params=_cp("parallel", "parallel", "parallel"),
        name="dilated_attention",
    )(qk, qk, qk, qk, vm, vm, vm)


def _mem_att_kernel(q_ref, k_ref, v_ref, o_ref):
    s = lax.dot_general(q_ref[...], k_ref[...], (((1,), (1,)), ((), ())),
                        preferred_element_type=F32) * (MEM_HEAD_DIM ** -0.5)
    p = jnp.exp(s - jnp.max(s, axis=-1, keepdims=True))
    vb = jnp.concatenate([v_ref[...], jnp.ones((N_MEM_TOKENS, LANES), BF16)], axis=1)
    pv = jnp.dot(p.astype(BF16), vb, preferred_element_type=F32)
    o_ref[...] = (pv[:, :LANES] / pv[:, LANES:]).astype(o_ref.dtype)


def _memory_attention(vm, kv, B, S, ts):
    n_t = S // ts
    return pl.pallas_call(
        _mem_att_kernel,
        grid=(B, N_MEM_HEADS, n_t),
        in_specs=[pl.BlockSpec((None, ts, LANES), lambda b, hd, i: (N_PAIRS + hd, b * n_t + i, 0)),
                  pl.BlockSpec((None, N_MEM_TOKENS, LANES), lambda b, hd, i: (b, 0, hd)),
                  pl.BlockSpec((None, N_MEM_TOKENS, LANES), lambda b, hd, i: (b, 0, N_MEM_HEADS + hd))],
        out_specs=pl.BlockSpec((None, ts, LANES), lambda b, hd, i: (hd, b * n_t + i, 0)),
        out_shape=jax.ShapeDtypeStruct((N_MEM_HEADS, B * S, LANES), BF16),
        compiler_params=_cp("parallel", "parallel", "parallel"),
        name="memory_attention",
    )(vm, kv, kv)


def _out_proj_kernel(c_ref, a_ref, m_ref, x_ref, w_ref, g_ref, b_ref, of_ref, ob_ref):
    half = x_ref.shape[0] // 2
    for r0 in (0, half):
        rows = pl.ds(r0, half)
        mixed = jnp.concatenate([c_ref[rows, :]] + [a_ref[j, rows, :] for j in range(N_PAIRS)]
                                + [m_ref[j, rows, :] for j in range(N_MEM_HEADS)], axis=1)
        y = jnp.dot(mixed, w_ref[...], preferred_element_type=F32)
        z = _layernorm_rows(ALPHA * x_ref[rows, :] + y, g_ref[...], b_ref[...])
        of_ref[rows, :] = z
        ob_ref[rows, :] = z.astype(BF16)


def _out_proj_ln(conv_o, att_o, mem_o, x, w_out, g, b, tm):
    T = x.shape[0]
    row = lambda w: pl.BlockSpec((tm, w), lambda i: (i, 0))
    slabs = lambda n: pl.BlockSpec((n, tm, LANES), lambda i: (0, i, 0))
    vec = pl.BlockSpec((1, D_MODEL), lambda i: (0, 0))
    return pl.pallas_call(
        _out_proj_kernel,
        grid=(T // tm,),
        in_specs=[row(D_CONV), slabs(N_PAIRS), slabs(N_MEM_HEADS), row(D_MODEL),
                  pl.BlockSpec((D_MODEL, D_MODEL), lambda i: (0, 0), pipeline_mode=pl.Buffered(1)),
                  vec, vec],
        out_specs=[row(D_MODEL), row(D_MODEL)],
        out_shape=[jax.ShapeDtypeStruct((T, D_MODEL), F32), jax.ShapeDtypeStruct((T, D_MODEL), BF16)],
        compiler_params=_cp("parallel"),
        name="out_proj_ln",
    )(conv_o, att_o, mem_o, x, w_out, g.reshape(1, -1), b.reshape(1, -1))


def _swiglu_step(xb, w1_ref, w3_ref, w2_ref):
    h1 = jnp.dot(xb, w1_ref[...], preferred_element_type=F32)
    h3 = jnp.dot(xb, w3_ref[...], preferred_element_type=F32)
    act = (h1 * jax.nn.sigmoid(h1) * h3).astype(BF16)
    return jnp.dot(act, w2_ref[...], preferred_element_type=F32)


CAST_STEPS = 8


def _ffn_kernel(*refs, n_cast):
    xb_ref, w1_ref, w3_ref, w2_ref, x_ref, g_ref, b_ref = refs[:7]
    src_refs = refs[7:7 + n_cast]
    of_ref, ob_ref = refs[7 + n_cast:9 + n_cast]
    dst_refs = refs[9 + n_cast:9 + 2 * n_cast]
    acc = refs[-1]
    f = pl.program_id(1)

    @pl.when(f == 0)
    def _():
        acc[...] = jnp.zeros_like(acc)

    acc[...] += _swiglu_step(xb_ref[...], w1_ref, w3_ref, w2_ref)

    @pl.when(f == pl.num_programs(1) - 1)
    def _():
        z = _layernorm_rows(ALPHA * x_ref[...] + acc[...], g_ref[...], b_ref[...])
        of_ref[...] = z
        ob_ref[...] = z.astype(BF16)

    if n_cast:
        @pl.when(f < CAST_STEPS)
        def _():
            for s_ref, d_ref in zip(src_refs, dst_refs):
                d_ref[...] = s_ref[...].astype(BF16)


def _ffn_ln(xb, x, w1, w3, w2, g, b, tm, tf, cast_srcs=()):
    T = x.shape[0]
    F = w1.shape[1]
    n_i, n_f = T // tm, F // tf
    row = pl.BlockSpec((tm, D_MODEL), lambda i, f: (i, 0))
    vec = pl.BlockSpec((1, D_MODEL), lambda i, f: (0, 0))
    n_blocks = n_i * CAST_STEPS
    cast_specs = []
    for a in cast_srcs:
        rows = a.shape[0] // n_blocks
        if n_f < CAST_STEPS or rows * n_blocks != a.shape[0] or rows % 16:
            raise ValueError("side-stream array does not split evenly over the grid")
        cast_specs.append(pl.BlockSpec(
            (rows, a.shape[1]), lambda i, f: (i * CAST_STEPS + jnp.minimum(f, CAST_STEPS - 1), 0)))
    outs = pl.pallas_call(
        functools.partial(_ffn_kernel, n_cast=len(cast_srcs)),
        grid=(n_i, n_f),
        in_specs=[row,
                  pl.BlockSpec((D_MODEL, tf), lambda i, f: (0, f)),
                  pl.BlockSpec((D_MODEL, tf), lambda i, f: (0, f)),
                  pl.BlockSpec((tf, D_MODEL), lambda i, f: (f, 0)),
                  row, vec, vec] + cast_specs,
        out_specs=[row, row] + cast_specs,
        out_shape=[jax.ShapeDtypeStruct((T, D_MODEL), F32), jax.ShapeDtypeStruct((T, D_MODEL), BF16)]
                  + [jax.ShapeDtypeStruct(a.shape, BF16) for a in cast_srcs],
        scratch_shapes=[pltpu.VMEM((tm, D_MODEL), F32)],
        compiler_params=_cp("arbitrary", "arbitrary"),
        name="dense_ffn_ln",
    )(xb, w1, w3, w2, x, g.reshape(1, -1), b.reshape(1, -1), *cast_srcs)
    return outs[0], outs[1], list(outs[2:])


def _split_bf16(x):
    hi = x.astype(BF16)
    return hi, (x - hi.astype(F32)).astype(BF16)


def _router_kernel(x_ref, r_ref, idx_ref, gate_ref):
    xh, xl = _split_bf16(x_ref[...])
    rh, rl = _split_bf16(r_ref[...])
    dot = functools.partial(jnp.dot, preferred_element_type=F32)
    logits = dot(xh, rh) + (dot(xl, rh) + dot(xh, rl))
    e = lax.broadcasted_iota(jnp.int32, logits.shape, 1)
    v1 = jnp.max(logits, axis=-1, keepdims=True)
    i1 = jnp.min(jnp.where(logits == v1, e, N_EXPERTS), axis=-1, keepdims=True)
    rest = jnp.where(e == i1, -jnp.inf, logits)
    v2 = jnp.max(rest, axis=-1, keepdims=True)
    i2 = jnp.min(jnp.where(rest == v2, e, N_EXPERTS), axis=-1, keepdims=True)
    e2 = jnp.exp(v2 - v1)
    denom = 1.0 + e2
    idx_ref[...] = jnp.concatenate([i1, i2], axis=1)
    gate_ref[...] = jnp.concatenate([1.0 / denom, e2 / denom], axis=1)


def _router(x, router, tm):
    T = x.shape[0]
    return pl.pallas_call(
        _router_kernel,
        grid=(T // tm,),
        in_specs=[pl.BlockSpec((tm, D_MODEL), lambda i: (i, 0)),
                  pl.BlockSpec((D_MODEL, N_EXPERTS), lambda i: (0, 0))],
        out_specs=[pl.BlockSpec((tm, 2), lambda i: (i, 0)), pl.BlockSpec((tm, 2), lambda i: (i, 0))],
        out_shape=[jax.ShapeDtypeStruct((T, 2), jnp.int32), jax.ShapeDtypeStruct((T, 2), F32)],
        compiler_params=_cp("parallel"),
        name="router_top2",
    )(x, router)


def _routing_plan(idx, tm):
    T = idx.shape[0]
    i32 = jnp.int32
    n_visits = 2 * T // tm + N_EXPERTS - 1
    flat = idx.reshape(-1)
    onehot = (flat[:, None] == jnp.arange(N_EXPERTS, dtype=i32)[None, :]).astype(i32)
    rank = jnp.cumsum(onehot, axis=0) - onehot
    counts = jnp.sum(onehot, axis=0)
    ends = jnp.cumsum(counts)
    starts = ends - counts
    pos = jnp.sum((rank + starts[None, :]) * onehot, axis=1).astype(i32)
    first_tile = starts // tm
    n_e = jnp.where(counts > 0, (ends - 1) // tm - first_tile + 1, 0)
    v_end = jnp.cumsum(n_e)
    v_start = v_end - n_e
    v = jnp.arange(n_visits, dtype=i32)
    valid = v < v_end[-1]
    vc = jnp.minimum(v, v_end[-1] - 1)
    e = jnp.minimum(jnp.sum(vc[:, None] >= v_end[None, :], axis=1), N_EXPERTS - 1).astype(i32)
    tile = first_tile[e] + (vc - v_start[e])
    lo = jnp.maximum(starts[e], tile * tm) - tile * tm
    hi = jnp.minimum(ends[e], (tile + 1) * tm) - tile * tm
    first = jnp.concatenate([jnp.ones((1,), bool), tile[1:] != tile[:-1]])
    table = [tile, e, lo, jnp.where(valid, hi, lo), first & valid, valid]
    return pos, [a.astype(i32) for a in table]


def _dispatch_kernel(pos_ref, x_ref, o_hbm, sem, *, td):
    def copy(j, k):
        return pltpu.make_async_copy(x_ref.at[pl.ds(j, 1)],
                                     o_hbm.at[pl.ds(pos_ref[0, 2 * j + k], 1)], sem)

    def start(j, carry):
        copy(j, 0).start()
        copy(j, 1).start()
        return carry

    def wait(j, carry):
        copy(j, 0).wait()
        copy(j, 1).wait()
        return carry

    lax.fori_loop(0, td, start, 0, unroll=8)
    lax.fori_loop(0, td, wait, 0, unroll=8)


def _dispatch(x, pos, td):
    T = x.shape[0]
    return pl.pallas_call(
        functools.partial(_dispatch_kernel, td=td),
        grid=(T // td,),
        in_specs=[pl.BlockSpec((None, 1, 2 * td), lambda i: (i, 0, 0), memory_space=pltpu.SMEM),
                  pl.BlockSpec((td, D_MODEL), lambda i: (i, 0))],
        out_specs=pl.BlockSpec(memory_space=pl.ANY),
        out_shape=jax.ShapeDtypeStruct((2 * T, D_MODEL), F32),
        scratch_shapes=[pltpu.SemaphoreType.DMA],
        compiler_params=_cp("arbitrary"),
        name="expert_dispatch",
    )(pos.reshape(T // td, 1, 2 * td), x)


def _moe_ffn_kernel(tile_ref, exp_ref, lo_ref, hi_ref, first_ref, valid_ref,
                    x_ref, w1_ref, w3_ref, w2_ref, o_ref, xb, acc):
    del tile_ref, exp_ref
    v = pl.program_id(0)
    f = pl.program_id(1)
    valid = valid_ref[v] > 0

    @pl.when(valid & (f == 0))
    def _():
        row = lax.broadcasted_iota(jnp.int32, x_ref.shape, 0)
        mine = (row >= lo_ref[v]) & (row < hi_ref[v])
        xb[...] = jnp.where(mine, x_ref[...], 0.0).astype(BF16)

    @pl.when((first_ref[v] > 0) & (f == 0))
    def _():
        acc[...] = jnp.zeros_like(acc)

    @pl.when(valid)
    def _():
        acc[...] += _swiglu_step(xb[...], w1_ref, w3_ref, w2_ref)

    @pl.when(valid & (f == pl.num_programs(1) - 1))
    def _():
        o_ref[...] = acc[...]


def _moe_ffn(xs, table, w1, w3, w2, tm, tf):
    n_rows = xs.shape[0]
    n_visits = table[0].shape[0]
    nf = w1.shape[2] // tf

    def fidx(v, f, valid):
        return jnp.where(valid[v] > 0, f, nf - 1)

    row = pl.BlockSpec((tm, D_MODEL), lambda v, f, tile, *_: (tile[v], 0))
    up = pl.BlockSpec((None, D_MODEL, tf),
                      lambda v, f, tile, e, lo, hi, first, valid: (e[v], 0, fidx(v, f, valid)))
    down = pl.BlockSpec((None, tf, D_MODEL),
                        lambda v, f, tile, e, lo, hi, first, valid: (e[v], fidx(v, f, valid), 0))
    grid_spec = pltpu.PrefetchScalarGridSpec(
        num_scalar_prefetch=len(table),
        grid=(n_visits, nf),
        in_specs=[row, up, up, down],
        out_specs=row,
        scratch_shapes=[pltpu.VMEM((tm, D_MODEL), BF16), pltpu.VMEM((tm, D_MODEL), F32)],
    )
    return pl.pallas_call(
        _moe_ffn_kernel,
        grid_spec=grid_spec,
        out_shape=jax.ShapeDtypeStruct((n_rows, D_MODEL), F32),
        compiler_params=_cp("arbitrary", "arbitrary"),
        name="expert_ffn",
    )(*table, xs, w1, w3, w2)


def _combine_kernel(pos_ref, pos_next_ref, gate_ref, x_ref, g_ref, b_ref, y_hbm, of_ref, ob_ref,
                    ybuf, sem, *, tc):
    i = pl.program_id(0)
    slot = i % 2

    def copy(p_ref, s, j, k):
        return pltpu.make_async_copy(y_hbm.at[pl.ds(p_ref[0, 2 * j + k], 1)],
                                     ybuf.at[s, k, pl.ds(j, 1)], sem.at[s])

    def start_all(p_ref, s):
        def body(j, carry):
            copy(p_ref, s, j, 0).start()
            copy(p_ref, s, j, 1).start()
            return carry
        lax.fori_loop(0, tc, body, 0, unroll=8)

    def wait_all(p_ref, s):
        def body(j, carry):
            copy(p_ref, s, j, 0).wait()
            copy(p_ref, s, j, 1).wait()
            return carry
        lax.fori_loop(0, tc, body, 0, unroll=8)

    @pl.when(i == 0)
    def _():
        start_all(pos_ref, 0)

    @pl.when(i + 1 < pl.num_programs(0))
    def _():
        start_all(pos_next_ref, 1 - slot)

    wait_all(pos_ref, slot)
    gate = gate_ref[...]
    f = gate[:, 0:1] * ybuf[slot, 0] + gate[:, 1:2] * ybuf[slot, 1]
    z = _layernorm_rows(ALPHA * x_ref[...] + f, g_ref[...], b_ref[...])
    of_ref[...] = z
    ob_ref[...] = z.astype(BF16)


def _combine_ln(ys, pos, gates, x, g, b, tc):
    T = x.shape[0]
    n = T // tc
    row = pl.BlockSpec((tc, D_MODEL), lambda i: (i, 0))
    vec = pl.BlockSpec((1, D_MODEL), lambda i: (0, 0))
    pos3 = pos.reshape(n, 1, 2 * tc)
    return pl.pallas_call(
        functools.partial(_combine_kernel, tc=tc),
        grid=(n,),
        in_specs=[pl.BlockSpec((None, 1, 2 * tc), lambda i: (i, 0, 0), memory_space=pltpu.SMEM),
                  pl.BlockSpec((None, 1, 2 * tc), lambda i: (jnp.minimum(i + 1, n - 1), 0, 0),
                               memory_space=pltpu.SMEM),
                  pl.BlockSpec((tc, 2), lambda i: (i, 0)),
                  row, vec, vec,
                  pl.BlockSpec(memory_space=pl.ANY)],
        out_specs=[row, row],
        out_shape=[jax.ShapeDtypeStruct((T, D_MODEL), F32), jax.ShapeDtypeStruct((T, D_MODEL), BF16)],
        scratch_shapes=[pltpu.VMEM((2, 2, tc, D_MODEL), F32), pltpu.SemaphoreType.DMA((2,))],
        compiler_params=_cp("arbitrary"),
        name="expert_combine_ln",
    )(pos3, pos3, gates, x, g.reshape(1, -1), b.reshape(1, -1), ys)


def _tiles(B, S):
    T = B * S
    return dict(
        tm_proj=min(512, T), tm_kv=min(512, B * N_MEM_TOKENS),
        ts_conv=min(256, S), att_chunk=min(4096, S), ts_mem=min(1024, S),
        tm_out=min(512, T), tm_ffn=min(512, T), tf=512, tf_moe=1024,
        tm_route=min(512, T), td=min(256, T), tm_moe=min(512, T), tc=min(256, T),
    )


class _Group:
    def __init__(self, x, mem):
        self.B, self.S, _ = x.shape
        self.T = self.B * self.S
        if self.S % (DILATIONS[-1] * ATT_TQ) or self.S % ATT_HALO:
            raise ValueError("sequence length must be a multiple of 2048")
        self.t = _tiles(self.B, self.S)
        self.rope_tabs = _rope_tables(self.S)
        self.x = x.reshape(self.T, D_MODEL)
        self.xb = self.x
        self.mem_b = mem.reshape(self.B * N_MEM_TOKENS, D_MODEL).astype(BF16)


def _mixer(g, l, p, wb):
    B, S, T, t = g.B, g.S, g.T, g.t
    ag, qk, vm = _in_proj(g.xb, wb["w_in"][l], g.rope_tabs, t["tm_proj"])
    kv = _matmul(g.mem_b, wb["w_mem_kv"][l], t["tm_kv"], 2 * D_MEM, BF16)
    kv = kv.reshape(B, N_MEM_TOKENS, 2 * D_MEM)
    conv_o = _conformer_conv(ag.reshape(B, S, 2 * D_CONV), p["conv_w"][l], p["conv_b"][l],
                             p["conv_ln_g"][l], p["conv_ln_b"][l], t["ts_conv"])
    att_o = _dilated_attention(qk, vm, B, S, t["att_chunk"])
    mem_o = _memory_attention(vm, kv, B, S, t["ts_mem"])
    g.x, g.xb = _out_proj_ln(conv_o.reshape(T, D_CONV), att_o, mem_o, g.x, wb["w_out"][l],
                             p["ln1_g"][l], p["ln1_b"][l], t["tm_out"])


def _moe(g, l, p, moe_w):
    t = g.t
    idx, gates = _router(g.x, p["moe_router"][l // 2], t["tm_route"])
    pos, table = _routing_plan(idx, t["tm_moe"])
    xs = _dispatch(g.x, pos, t["td"])
    ys = _moe_ffn(xs, table, *moe_w, t["tm_moe"], t["tf_moe"])
    g.x, g.xb = _combine_ln(ys, pos, gates, g.x, p["ln2_g"][l], p["ln2_b"][l], t["tc"])


def kernel(x_prompt, x_sample, mem_prompt, mem_sample, w_in, conv_w, conv_b, conv_ln_g, conv_ln_b,
           w_mem_kv, w_out, ln1_g, ln1_b, ffn_w1, ffn_w3, ffn_w2, moe_router, moe_w1, moe_w3, moe_w2,
           ln2_g, ln2_b):
    p = dict(conv_w=conv_w, conv_b=conv_b, conv_ln_g=conv_ln_g, conv_ln_b=conv_ln_b,
             ln1_g=ln1_g, ln1_b=ln1_b, ln2_g=ln2_g, ln2_b=ln2_b, moe_router=moe_router)
    wb = dict(w_in=w_in, w_mem_kv=w_mem_kv, w_out=w_out, ffn_w1=ffn_w1, ffn_w3=ffn_w3, ffn_w2=ffn_w2)
    wb = {k: v.astype(BF16) for k, v in wb.items()}
    moe_f32 = (moe_w1, moe_w3, moe_w2)
    groups = [_Group(x_prompt, mem_prompt), _Group(x_sample, mem_sample)]
    host = max(groups, key=lambda g: g.T)
    moe_b = {}
    for l in range(DEPTH):
        i = l // 2
        for g in groups:
            _mixer(g, l, p, wb)
            if l % 2 == 0:
                nxt = (l + 1) // 2
                srcs = ()
                if g is host and l + 1 < DEPTH:
                    srcs = [w[nxt].reshape(-1, w.shape[-1]) for w in moe_f32]
                g.x, g.xb, casts = _ffn_ln(g.xb, g.x, wb["ffn_w1"][i], wb["ffn_w3"][i], wb["ffn_w2"][i],
                                           p["ln2_g"][l], p["ln2_b"][l], g.t["tm_ffn"], g.t["tf"], srcs)
                if casts:
                    moe_b[nxt] = [c.reshape(w.shape[1:]) for c, w in zip(casts, moe_f32)]
            else:
                if i not in moe_b:
                    moe_b[i] = [w[i].astype(BF16) for w in moe_f32]
                _moe(g, l, p, moe_b[i])
    return tuple(g.x.reshape(g.B, g.S, D_MODEL) for g in groups)
```

```python
import functools
import math

import jax
import jax.numpy as jnp
import numpy as np
from jax import lax
from jax.experimental import pallas as pl
from jax.experimental.pallas import tpu as pltpu

F32 = jnp.float32
BF16 = jnp.bfloat16

D_MODEL = 2048
DEPTH = 2
D_CONV = 768
D_ATT = 768
D_MEM = 512
ATT_HEAD_DIM = 64
N_MEM_HEADS = 4
MEM_HEAD_DIM = 128
N_MEM_TOKENS = 256
D_IN = 2 * D_CONV + 3 * D_ATT + D_MEM
CONV_WIDTH = 31
CONV_PAD = CONV_WIDTH // 2
DILATIONS = (1, 4, 16)
N_SIDE = 64
ROT_DIM = 16
ROPE_THETA = 500000.0
N_EXPERTS = 8
ALPHA = (2 * DEPTH) ** 0.25
LN_EPS = 1e-5
NEG_INF = -1e30

LANES = 128
SUBLANES = 8
N_PAIRS = D_ATT // LANES

VMEM_LIMIT = 56 * 1024 * 1024


def _cp(*sem):
    return pltpu.CompilerParams(dimension_semantics=sem, vmem_limit_bytes=VMEM_LIMIT)


def _layernorm_rows(z, g, b):
    mu = jnp.mean(z, axis=-1, keepdims=True)
    zc = z - mu
    var = jnp.mean(zc * zc, axis=-1, keepdims=True)
    return zc * lax.rsqrt(var + LN_EPS) * g + b


def _matmul_kernel(x_ref, w_ref, o_ref):
    o_ref[...] = jnp.dot(x_ref[...].astype(BF16), w_ref[...],
                         preferred_element_type=F32).astype(o_ref.dtype)


def _matmul(x, w, tm, tn, out_dtype):
    M, K = x.shape
    N = w.shape[1]
    return pl.pallas_call(
        _matmul_kernel,
        grid=(N // tn, M // tm),
        in_specs=[pl.BlockSpec((tm, K), lambda j, i: (i, 0)),
                  pl.BlockSpec((K, tn), lambda j, i: (0, j))],
        out_specs=pl.BlockSpec((tm, tn), lambda j, i: (i, j)),
        out_shape=jax.ShapeDtypeStruct((M, N), out_dtype),
        compiler_params=_cp("parallel", "parallel"),
        name="proj_matmul",
    )(x, w)


CONV_HALO = 16
CONV_ROWS = 32


def _conv_kernel(ap_ref, am_ref, an_ref, gp_ref, gm_ref, gn_ref, w_ref, cb_ref, lg_ref, lb_ref,
                 o_ref, u_ref, ush_ref, y_ref, *, ts):
    i = pl.program_id(1)
    last = pl.num_programs(1) - 1

    def glu(a_ref, g_ref):
        return a_ref[...].astype(F32) * jax.nn.sigmoid(g_ref[...].astype(F32))

    u_ref[pl.ds(CONV_HALO, ts), :] = glu(am_ref, gm_ref)
    u_ref[pl.ds(0, CONV_HALO), :] = jnp.where(i > 0, glu(ap_ref, gp_ref), 0.0)
    u_ref[pl.ds(CONV_HALO + ts, CONV_HALO), :] = jnp.where(i < last, glu(an_ref, gn_ref), 0.0)

    cb = cb_ref[...]
    lg = lg_ref[...]
    lb = lb_ref[...]

    def fill(c, carry):
        r = pl.multiple_of(c * SUBLANES, SUBLANES)
        two = u_ref[pl.ds(r, 2 * SUBLANES), :]
        for m in range(SUBLANES):
            ush_ref[m, pl.ds(r, SUBLANES), :] = two[m:m + SUBLANES]
        return carry

    lax.fori_loop(0, (ts + 2 * CONV_HALO) // SUBLANES - 1, fill, 0)

    n_sub = CONV_ROWS // SUBLANES

    def conv(c, carry):
        r0 = pl.multiple_of(c * CONV_ROWS, CONV_ROWS)
        acc = [jnp.zeros((SUBLANES, D_CONV), F32) for _ in range(n_sub)]
        for t in range(CONV_WIDTH):
            q, m = divmod(t + CONV_HALO - CONV_PAD, SUBLANES)
            w_t = w_ref[t]
            for k in range(n_sub):
                acc[k] = acc[k] + w_t * ush_ref[m, pl.ds(r0 + (q + k) * SUBLANES, SUBLANES), :]
        for k in range(n_sub):
            y_ref[pl.ds(r0 + k * SUBLANES, SUBLANES), :] = acc[k] + cb
        return carry

    lax.fori_loop(0, ts // CONV_ROWS, conv, 0)

    def norm(c, carry):
        r0 = pl.multiple_of(c * CONV_HALO, CONV_HALO)
        y = _layernorm_rows(y_ref[pl.ds(r0, CONV_HALO), :], lg, lb)
        o_ref[pl.ds(r0, CONV_HALO), :] = (y * jax.nn.sigmoid(y)).astype(o_ref.dtype)
        return carry

    lax.fori_loop(0, ts // CONV_HALO, norm, 0, unroll=8)


def _conformer_conv(h, conv_w, conv_b, ln_g, ln_b, ts):
    B, S, _ = h.shape
    nh = ts // CONV_HALO
    n_halo = S // CONV_HALO
    w_b = jnp.broadcast_to(conv_w[:, None, :], (CONV_WIDTH, SUBLANES, D_CONV)).astype(F32)

    def main(col):
        return pl.BlockSpec((None, ts, D_CONV), lambda b, i: (b, i, col))

    def prev(col):
        return pl.BlockSpec((None, CONV_HALO, D_CONV),
                            lambda b, i: (b, jnp.maximum(i * nh - 1, 0), col))

    def nxt(col):
        return pl.BlockSpec((None, CONV_HALO, D_CONV),
                            lambda b, i: (b, jnp.minimum((i + 1) * nh, n_halo - 1), col))

    vec = pl.BlockSpec((1, D_CONV), lambda b, i: (0, 0))
    return pl.pallas_call(
        functools.partial(_conv_kernel, ts=ts),
        grid=(B, S // ts),
        in_specs=[prev(0), main(0), nxt(0), prev(1), main(1), nxt(1),
                  pl.BlockSpec((CONV_WIDTH, SUBLANES, D_CONV), lambda b, i: (0, 0, 0)),
                  vec, vec, vec],
        out_specs=pl.BlockSpec((None, ts, D_CONV), lambda b, i: (b, i, 0)),
        out_shape=jax.ShapeDtypeStruct((B, S, D_CONV), BF16),
        scratch_shapes=[pltpu.VMEM((ts + 2 * CONV_HALO, D_CONV), F32),
                        pltpu.VMEM((SUBLANES, ts + 2 * CONV_HALO - SUBLANES, D_CONV), F32),
                        pltpu.VMEM((ts, D_CONV), F32)],
        compiler_params=_cp("parallel", "parallel"),
        name="conformer_conv",
    )(h, h, h, h, h, h, w_b, conv_b.reshape(1, -1), ln_g.reshape(1, -1), ln_b.reshape(1, -1))


def _rope_tables(S):
    half = ROT_DIM // 2
    inv_freq = jnp.power(ROPE_THETA, -jnp.arange(0, ROT_DIM, 2, dtype=F32) / ROT_DIM)
    ang = jnp.arange(S, dtype=F32)[:, None] * inv_freq[None, :]
    cos, sin = jnp.cos(ang), jnp.sin(ang)
    zeros = jnp.zeros((S, ATT_HEAD_DIM - ROT_DIM), F32)
    z8 = jnp.zeros((S, half), F32)
    c_head = jnp.concatenate([cos, cos, zeros + 1.0], axis=1)
    up_head = jnp.concatenate([-sin, z8, zeros], axis=1)
    dn_head = jnp.concatenate([z8, sin, zeros], axis=1)
    tabs = jnp.stack([jnp.tile(t, (1, LANES // ATT_HEAD_DIM)) for t in (c_head, up_head, dn_head)])
    q_scale = ATT_HEAD_DIM ** -0.5 * math.log2(math.e)
    return jnp.stack([tabs * q_scale, tabs])


def _in_proj_kernel(x_ref, w_ref, t_ref, ag_ref, qk_ref, vm_ref):
    xb = x_ref[...].astype(BF16)
    n_ag, n_qk = 2 * D_CONV, 2 * D_ATT
    ag_ref[...] = jnp.dot(xb, w_ref[:, pl.ds(0, n_ag)], preferred_element_type=F32).astype(BF16)
    y = jnp.dot(xb, w_ref[:, pl.ds(n_ag, n_qk)], preferred_element_type=F32)
    for j in range(qk_ref.shape[0]):
        x = y[:, j * LANES:(j + 1) * LANES]
        t = t_ref[j // N_PAIRS]
        up = pltpu.roll(x, LANES - ROT_DIM // 2, 1)
        dn = pltpu.roll(x, ROT_DIM // 2, 1)
        qk_ref[j] = (x * t[0] + up * t[1] + dn * t[2]).astype(BF16)
    y = jnp.dot(xb, w_ref[:, pl.ds(n_ag + n_qk, D_ATT + D_MEM)], preferred_element_type=F32)
    for j in range(vm_ref.shape[0]):
        vm_ref[j] = y[:, j * LANES:(j + 1) * LANES].astype(BF16)


def _in_proj(x, w, rope_tabs, tm):
    M, K = x.shape
    n_pos = rope_tabs.shape[2] // tm
    n_qk = 2 * D_ATT // LANES
    n_vm = (D_ATT + D_MEM) // LANES
    slabs = lambda n: pl.BlockSpec((n, tm, LANES), lambda i: (0, i, 0))
    return pl.pallas_call(
        _in_proj_kernel,
        grid=(M // tm,),
        in_specs=[pl.BlockSpec((tm, K), lambda i: (i, 0)),
                  pl.BlockSpec((K, D_IN), lambda i: (0, 0)),
                  pl.BlockSpec((2, 3, tm, LANES), lambda i: (0, 0, i % n_pos, 0))],
        out_specs=[pl.BlockSpec((tm, 2 * D_CONV), lambda i: (i, 0)), slabs(n_qk), slabs(n_vm)],
        out_shape=[jax.ShapeDtypeStruct((M, 2 * D_CONV), BF16),
                   jax.ShapeDtypeStruct((n_qk, M, LANES), BF16),
                   jax.ShapeDtypeStruct((n_vm, M, LANES), BF16)],
        compiler_params=_cp("parallel"),
        name="in_proj",
    )(x, w, rope_tabs)


ATT_TQ = 128
ATT_TK = ATT_TQ + 2 * N_SIDE
ATT_HALO = N_SIDE * DILATIONS[-1]
ATT_BLOCKS_PER_STEP = 8


def _att_kernel(q_ref, kp_ref, km_ref, kn_ref, vp_ref, vm_ref, vn_ref, o_ref,
                qf, kf, vf, acc, den, ma, mb, bias_ref, *, C, S):
    c = pl.program_id(2)
    H = ATT_HALO
    qf[...] = q_ref[...].astype(F32)
    kf[pl.ds(0, H), :] = kp_ref[...].astype(F32)
    kf[pl.ds(H, C), :] = km_ref[...].astype(F32)
    kf[pl.ds(H + C, H), :] = kn_ref[...].astype(F32)
    vf[pl.ds(0, H), :] = vp_ref[...].astype(F32)
    vf[pl.ds(H, C), :] = vm_ref[...].astype(F32)
    vf[pl.ds(H + C, H), :] = vn_ref[...].astype(F32)

    lane = lax.broadcasted_iota(jnp.int32, (ATT_TQ, LANES), 1)
    first_head = lane < ATT_HEAD_DIM
    qi = lax.broadcasted_iota(jnp.int32, (ATT_TQ, ATT_TK), 0)
    kj = lax.broadcasted_iota(jnp.int32, (ATT_TQ, ATT_TK), 1)
    band = jnp.abs(kj - qi - N_SIDE) <= N_SIDE
    ones = jnp.ones((ATT_TK, LANES), BF16)
    for case in range(4):
        ok = band
        if case & 1:
            ok = ok & (kj >= N_SIDE)
        if case & 2:
            ok = ok & (kj < ATT_TK - N_SIDE)
        bias_ref[case] = jnp.where(ok, 0.0, NEG_INF)

    for d in reversed(DILATIONS):
        n_blk = C // (d * ATT_TQ)
        seq_len = S // d
        init = d == DILATIONS[-1]

        def rows_of(t, d=d, n_blk=n_blk):
            r = t // n_blk
            ib = t % n_blk
            q0 = r + d * ib * ATT_TQ
            k0 = H + r + d * (ib * ATT_TQ - N_SIDE)
            if d == 1:
                return ib, pl.ds(q0, ATT_TQ), pl.ds(k0, ATT_TK)
            return ib, pl.ds(q0, ATT_TQ, stride=d), pl.ds(k0, ATT_TK, stride=d)

        def load(t, init=init):
            ib, rows_q, rows_k = rows_of(t)
            x = dict(ib=ib, q=qf[rows_q, :], k=kf[rows_k, :], v=vf[rows_k, :])
            if not init:
                x.update(ma=ma[rows_q, :], mb=mb[rows_q, :], acc=acc[rows_q, :], den=den[rows_q, :])
            return x

        def compute(x, d=d, seq_len=seq_len, init=init):
            kb = x["k"].astype(BF16)
            vb = jnp.concatenate([x["v"].astype(BF16), ones], axis=1)
            g0 = c * (C // d) + x["ib"] * ATT_TQ - N_SIDE
            case = (g0 < 0).astype(jnp.int32) + 2 * (g0 + ATT_TK > seq_len).astype(jnp.int32)
            bias = bias_ref[case]

            def head(q_h, m_old):
                s = lax.dot_general(q_h.astype(BF16), kb, (((1,), (1,)), ((), ())),
                                    preferred_element_type=F32)
                s = s + bias
                m_new = jnp.max(s, axis=-1, keepdims=True)
                if m_old is not None:
                    m_new = jnp.maximum(m_old, m_new)
                else:
                    m_new = jnp.broadcast_to(m_new, (ATT_TQ, LANES))
                p = jnp.exp2(s - jnp.concatenate([m_new, m_new], axis=1))
                pv = jnp.dot(p.astype(BF16), vb, preferred_element_type=F32)
                return pv, m_new

            pv_a, ma_new = head(jnp.where(first_head, x["q"], 0.0), x.get("ma"))
            pv_b, mb_new = head(jnp.where(first_head, 0.0, x["q"]), x.get("mb"))
            num = jnp.where(first_head, pv_a[:, :LANES], pv_b[:, :LANES])
            rsum = jnp.where(first_head, pv_a[:, LANES:], pv_b[:, LANES:])
            if init:
                return dict(ma=ma_new, mb=mb_new, acc=num, den=rsum)
            scale = jnp.where(first_head, jnp.exp2(x["ma"] - ma_new), jnp.exp2(x["mb"] - mb_new))
            return dict(ma=ma_new, mb=mb_new, acc=x["acc"] * scale + num, den=x["den"] * scale + rsum)

        def store(t, y):
            _, rows_q, _ = rows_of(t)
            ma[rows_q, :] = y["ma"]
            mb[rows_q, :] = y["mb"]
            acc[rows_q, :] = y["acc"]
            den[rows_q, :] = y["den"]

        def step(i, carry):
            ts = [i * ATT_BLOCKS_PER_STEP + u for u in range(ATT_BLOCKS_PER_STEP)]
            ys = [compute(x) for x in [load(t) for t in ts]]
            for t, y in zip(ts, ys):
                store(t, y)
            return carry

        lax.fori_loop(0, d * n_blk // ATT_BLOCKS_PER_STEP, step, 0)

    def finish(t, carry):
        r0 = pl.multiple_of(t * ATT_TQ, ATT_TQ)
        rows = pl.ds(r0, ATT_TQ)
        o_ref[rows, :] = (acc[rows, :] / den[rows, :]).astype(o_ref.dtype)
        return carry

    lax.fori_loop(0, C // ATT_TQ, finish, 0)


def _dilated_attention(qk, vm, B, S, C):
    H = ATT_HALO
    n_chunk = S // C
    nh = C // H
    n_halo = S // H

    def main(blk0):
        return pl.BlockSpec((None, C, LANES), lambda b, p, c: (blk0 + p, b * n_chunk + c, 0))

    def prev(blk0):
        return pl.BlockSpec((None, H, LANES),
                            lambda b, p, c: (blk0 + p, b * n_halo + jnp.maximum(c * nh - 1, 0), 0))

    def nxt(blk0):
        return pl.BlockSpec((None, H, LANES),
                            lambda b, p, c: (blk0 + p, b * n_halo + jnp.minimum((c + 1) * nh, n_halo - 1), 0))

    stat = pltpu.VMEM((C, LANES), F32)
    window = pltpu.VMEM((C + 2 * H, LANES), F32)
    return pl.pallas_call(
        functools.partial(_att_kernel, C=C, S=S),
        grid=(B, N_PAIRS, n_chunk),
        in_specs=[main(0), prev(N_PAIRS), main(N_PAIRS), nxt(N_PAIRS), prev(0), main(0), nxt(0)],
        out_specs=main(0),
        out_shape=jax.ShapeDtypeStruct((N_PAIRS, B * S, LANES), BF16),
        scratch_shapes=[stat, window, window, stat, stat, stat, stat,
                        pltpu.VMEM((4, ATT_TQ, ATT_TK), F32)],
        compiler_params=_cp("parallel", "parallel", "parallel"),
        name="dilated_attention",
    )(qk, qk, qk, qk, vm, vm, vm)


def _mem_att_kernel(q_ref, k_ref, v_ref, o_ref):
    s = lax.dot_general(q_ref[...], k_ref[...], (((1,), (1,)), ((), ())),
                        preferred_element_type=F32) * (MEM_HEAD_DIM ** -0.5)
    p = jnp.exp(s - jnp.max(s, axis=-1, keepdims=True))
    vb = jnp.concatenate([v_ref[...], jnp.ones((N_MEM_TOKENS, LANES), BF16)], axis=1)
    pv = jnp.dot(p.astype(BF16), vb, preferred_element_type=F32)
    o_ref[...] = (pv[:, :LANES] / pv[:, LANES:]).astype(o_ref.dtype)


def _memory_attention(vm, kv, B, S, ts):
    n_t = S // ts
    return pl.pallas_call(
        _mem_att_kernel,
        grid=(B, N_MEM_HEADS, n_t),
        in_specs=[pl.BlockSpec((None, ts, LANES), lambda b, hd, i: (N_PAIRS + hd, b * n_t + i, 0)),
                  pl.BlockSpec((None, N_MEM_TOKENS, LANES), lambda b, hd, i: (b, 0, hd)),
                  pl.BlockSpec((None, N_MEM_TOKENS, LANES), lambda b, hd, i: (b, 0, N_MEM_HEADS + hd))],
        out_specs=pl.BlockSpec((None, ts, LANES), lambda b, hd, i: (hd, b * n_t + i, 0)),
        out_shape=jax.ShapeDtypeStruct((N_MEM_HEADS, B * S, LANES), BF16),
        compiler_params=_cp("parallel", "parallel", "parallel"),
        name="memory_attention",
    )(vm, kv, kv)


def _out_proj_kernel(c_ref, a_ref, m_ref, x_ref, w_ref, g_ref, b_ref, of_ref, ob_ref):
    half = x_ref.shape[0] // 2
    for r0 in (0, half):
        rows = pl.ds(r0, half)
        mixed = jnp.concatenate([c_ref[rows, :]] + [a_ref[j, rows, :] for j in range(N_PAIRS)]
                                + [m_ref[j, rows, :] for j in range(N_MEM_HEADS)], axis=1)
        y = jnp.dot(mixed, w_ref[...], preferred_element_type=F32)
        z = _layernorm_rows(ALPHA * x_ref[rows, :] + y, g_ref[...], b_ref[...])
        of_ref[rows, :] = z
        ob_ref[rows, :] = z.astype(BF16)


def _out_proj_ln(conv_o, att_o, mem_o, x, w_out, g, b, tm):
    T = x.shape[0]
    row = lambda w: pl.BlockSpec((tm, w), lambda i: (i, 0))
    slabs = lambda n: pl.BlockSpec((n, tm, LANES), lambda i: (0, i, 0))
    vec = pl.BlockSpec((1, D_MODEL), lambda i: (0, 0))
    return pl.pallas_call(
        _out_proj_kernel,
        grid=(T // tm,),
        in_specs=[row(D_CONV), slabs(N_PAIRS), slabs(N_MEM_HEADS), row(D_MODEL),
                  pl.BlockSpec((D_MODEL, D_MODEL), lambda i: (0, 0), pipeline_mode=pl.Buffered(1)),
                  vec, vec],
        out_specs=[row(D_MODEL), row(D_MODEL)],
        out_shape=[jax.ShapeDtypeStruct((T, D_MODEL), F32), jax.ShapeDtypeStruct((T, D_MODEL), BF16)],
        compiler_params=_cp("parallel"),
        name="out_proj_ln",
    )(conv_o, att_o, mem_o, x, w_out, g.reshape(1, -1), b.reshape(1, -1))


def _swiglu_step(xb, w1_ref, w3_ref, w2_ref):
    h1 = jnp.dot(xb, w1_ref[...], preferred_element_type=F32)
    h3 = jnp.dot(xb, w3_ref[...], preferred_element_type=F32)
    act = (h1 * jax.nn.sigmoid(h1) * h3).astype(BF16)
    return jnp.dot(act, w2_ref[...], preferred_element_type=F32)


CAST_STEPS = 8


def _ffn_kernel(*refs, n_cast):
    xb_ref, w1_ref, w3_ref, w2_ref, x_ref, g_ref, b_ref = refs[:7]
    src_refs = refs[7:7 + n_cast]
    of_ref, ob_ref = refs[7 + n_cast:9 + n_cast]
    dst_refs = refs[9 + n_cast:9 + 2 * n_cast]
    acc = refs[-1]
    f = pl.program_id(1)

    @pl.when(f == 0)
    def _():
        acc[...] = jnp.zeros_like(acc)

    acc[...] += _swiglu_step(xb_ref[...], w1_ref, w3_ref, w2_ref)

    @pl.when(f == pl.num_programs(1) - 1)
    def _():
        z = _layernorm_rows(ALPHA * x_ref[...] + acc[...], g_ref[...], b_ref[...])
        of_ref[...] = z
        ob_ref[...] = z.astype(BF16)

    if n_cast:
        @pl.when(f < CAST_STEPS)
        def _():
            for s_ref, d_ref in zip(src_refs, dst_refs):
                d_ref[...] = s_ref[...].astype(BF16)


def _ffn_ln(xb, x, w1, w3, w2, g, b, tm, tf, cast_srcs=()):
    T = x.shape[0]
    F = w1.shape[1]
    n_i, n_f = T // tm, F // tf
    row = pl.BlockSpec((tm, D_MODEL), lambda i, f: (i, 0))
    vec = pl.BlockSpec((1, D_MODEL), lambda i, f: (0, 0))
    n_blocks = n_i * CAST_STEPS
    cast_specs = []
    for a in cast_srcs:
        rows = a.shape[0] // n_blocks
        if n_f < CAST_STEPS or rows * n_blocks != a.shape[0] or rows % 16:
            raise ValueError("side-stream array does not split evenly over the grid")
        cast_specs.append(pl.BlockSpec(
            (rows, a.shape[1]), lambda i, f: (i * CAST_STEPS + jnp.minimum(f, CAST_STEPS - 1), 0)))
    outs = pl.pallas_call(
        functools.partial(_ffn_kernel, n_cast=len(cast_srcs)),
        grid=(n_i, n_f),
        in_specs=[row,
                  pl.BlockSpec((D_MODEL, tf), lambda i, f: (0, f)),
                  pl.BlockSpec((D_MODEL, tf), lambda i, f: (0, f)),
                  pl.BlockSpec((tf, D_MODEL), lambda i, f: (f, 0)),
                  row, vec, vec] + cast_specs,
        out_specs=[row, row] + cast_specs,
        out_shape=[jax.ShapeDtypeStruct((T, D_MODEL), F32), jax.ShapeDtypeStruct((T, D_MODEL), BF16)]
                  + [jax.ShapeDtypeStruct(a.shape, BF16) for a in cast_srcs],
        scratch_shapes=[pltpu.VMEM((tm, D_MODEL), F32)],
        compiler_params=_cp("arbitrary", "arbitrary"),
        name="dense_ffn_ln",
    )(xb, w1, w3, w2, x, g.reshape(1, -1), b.reshape(1, -1), *cast_srcs)
    return outs[0], outs[1], list(outs[2:])


def _split_bf16(x):
    hi = x.astype(BF16)
    return hi, (x - hi.astype(F32)).astype(BF16)


def _router_kernel(x_ref, r_ref, idx_ref, gate_ref):
    xh, xl = _split_bf16(x_ref[...])
    rh, rl = _split_bf16(r_ref[...])
    dot = functools.partial(jnp.dot, preferred_element_type=F32)
    logits = dot(xh, rh) + (dot(xl, rh) + dot(xh, rl))
    e = lax.broadcasted_iota(jnp.int32, logits.shape, 1)
    v1 = jnp.max(logits, axis=-1, keepdims=True)
    i1 = jnp.min(jnp.where(logits == v1, e, N_EXPERTS), axis=-1, keepdims=True)
    rest = jnp.where(e == i1, -jnp.inf, logits)
    v2 = jnp.max(rest, axis=-1, keepdims=True)
    i2 = jnp.min(jnp.where(rest == v2, e, N_EXPERTS), axis=-1, keepdims=True)
    e2 = jnp.exp(v2 - v1)
    denom = 1.0 + e2
    idx_ref[...] = jnp.concatenate([i1, i2], axis=1)
    gate_ref[...] = jnp.concatenate([1.0 / denom, e2 / denom], axis=1)


def _router(x, router, tm):
    T = x.shape[0]
    return pl.pallas_call(
        _router_kernel,
        grid=(T // tm,),
        in_specs=[pl.BlockSpec((tm, D_MODEL), lambda i: (i, 0)),
                  pl.BlockSpec((D_MODEL, N_EXPERTS), lambda i: (0, 0))],
        out_specs=[pl.BlockSpec((tm, 2), lambda i: (i, 0)), pl.BlockSpec((tm, 2), lambda i: (i, 0))],
        out_shape=[jax.ShapeDtypeStruct((T, 2), jnp.int32), jax.ShapeDtypeStruct((T, 2), F32)],
        compiler_params=_cp("parallel"),
        name="router_top2",
    )(x, router)


def _routing_plan(idx, tm):
    T = idx.shape[0]
    i32 = jnp.int32
    n_visits = 2 * T // tm + N_EXPERTS - 1
    flat = idx.reshape(-1)
    onehot = (flat[:, None] == jnp.arange(N_EXPERTS, dtype=i32)[None, :]).astype(i32)
    rank = jnp.cumsum(onehot, axis=0) - onehot
    counts = jnp.sum(onehot, axis=0)
    ends = jnp.cumsum(counts)
    starts = ends - counts
    pos = jnp.sum((rank + starts[None, :]) * onehot, axis=1).astype(i32)
    first_tile = starts // tm
    n_e = jnp.where(counts > 0, (ends - 1) // tm - first_tile + 1, 0)
    v_end = jnp.cumsum(n_e)
    v_start = v_end - n_e
    v = jnp.arange(n_visits, dtype=i32)
    valid = v < v_end[-1]
    vc = jnp.minimum(v, v_end[-1] - 1)
    e = jnp.minimum(jnp.sum(vc[:, None] >= v_end[None, :], axis=1), N_EXPERTS - 1).astype(i32)
    tile = first_tile[e] + (vc - v_start[e])
    lo = jnp.maximum(starts[e], tile * tm) - tile * tm
    hi = jnp.minimum(ends[e], (tile + 1) * tm) - tile * tm
    first = jnp.concatenate([jnp.ones((1,), bool), tile[1:] != tile[:-1]])
    table = [tile, e, lo, jnp.where(valid, hi, lo), first & valid, valid]
    return pos, [a.astype(i32) for a in table]


def _dispatch_kernel(pos_ref, x_ref, o_hbm, sem, *, td):
    def copy(j, k):
        return pltpu.make_async_copy(x_ref.at[pl.ds(j, 1)],
                                     o_hbm.at[pl.ds(pos_ref[0, 2 * j + k], 1)], sem)

    def start(j, carry):
        copy(j, 0).start(priority=0)
        copy(j, 1).start(priority=1)
        return carry

    def wait(j, carry):
        copy(j, 0).wait()
        copy(j, 1).wait()
        return carry

    lax.fori_loop(0, td, start, 0, unroll=8)
    lax.fori_loop(0, td, wait, 0, unroll=8)


def _dispatch(x, pos, td):
    T = x.shape[0]
    return pl.pallas_call(
        functools.partial(_dispatch_kernel, td=td),
        grid=(T // td,),
        in_specs=[pl.BlockSpec((None, 1, 2 * td), lambda i: (i, 0, 0), memory_space=pltpu.SMEM),
                  pl.BlockSpec((td, D_MODEL), lambda i: (i, 0))],
        out_specs=pl.BlockSpec(memory_space=pl.ANY),
        out_shape=jax.ShapeDtypeStruct((2 * T, D_MODEL), F32),
        scratch_shapes=[pltpu.SemaphoreType.DMA],
        compiler_params=_cp("arbitrary"),
        name="expert_dispatch",
    )(pos.reshape(T // td, 1, 2 * td), x)


def _moe_ffn_kernel(tile_ref, exp_ref, lo_ref, hi_ref, first_ref, valid_ref,
                    x_ref, w1_ref, w3_ref, w2_ref, o_ref, xb, acc):
    del tile_ref, exp_ref
    v = pl.program_id(0)
    f = pl.program_id(1)
    valid = valid_ref[v] > 0

    @pl.when(valid & (f == 0))
    def _():
        row = lax.broadcasted_iota(jnp.int32, x_ref.shape, 0)
        mine = (row >= lo_ref[v]) & (row < hi_ref[v])
        xb[...] = jnp.where(mine, x_ref[...], 0.0).astype(BF16)

    @pl.when((first_ref[v] > 0) & (f == 0))
    def _():
        acc[...] = jnp.zeros_like(acc)

    @pl.when(valid)
    def _():
        acc[...] += _swiglu_step(xb[...], w1_ref, w3_ref, w2_ref)

    @pl.when(valid & (f == pl.num_programs(1) - 1))
    def _():
        o_ref[...] = acc[...]


def _moe_ffn(xs, table, w1, w3, w2, tm, tf):
    n_rows = xs.shape[0]
    n_visits = table[0].shape[0]
    nf = w1.shape[2] // tf

    def fidx(v, f, valid):
        return jnp.where(valid[v] > 0, f, nf - 1)

    row = pl.BlockSpec((tm, D_MODEL), lambda v, f, tile, *_: (tile[v], 0))
    up = pl.BlockSpec((None, D_MODEL, tf),
                      lambda v, f, tile, e, lo, hi, first, valid: (e[v], 0, fidx(v, f, valid)))
    down = pl.BlockSpec((None, tf, D_MODEL),
                        lambda v, f, tile, e, lo, hi, first, valid: (e[v], fidx(v, f, valid), 0))
    grid_spec = pltpu.PrefetchScalarGridSpec(
        num_scalar_prefetch=len(table),
        grid=(n_visits, nf),
        in_specs=[row, up, up, down],
        out_specs=row,
        scratch_shapes=[pltpu.VMEM((tm, D_MODEL), BF16), pltpu.VMEM((tm, D_MODEL), F32)],
    )
    return pl.pallas_call(
        _moe_ffn_kernel,
        grid_spec=grid_spec,
        out_shape=jax.ShapeDtypeStruct((n_rows, D_MODEL), F32),
        compiler_params=_cp("arbitrary", "arbitrary"),
        name="expert_ffn",
    )(*table, xs, w1, w3, w2)


def _combine_kernel(pos_ref, pos_next_ref, gate_ref, x_ref, g_ref, b_ref, y_hbm, of_ref, ob_ref,
                    ybuf, sem, *, tc):
    i = pl.program_id(0)
    slot = i % 2

    def copy(p_ref, s, j, k):
        return pltpu.make_async_copy(y_hbm.at[pl.ds(p_ref[0, 2 * j + k], 1)],
                                     ybuf.at[s, k, pl.ds(j, 1)], sem.at[s])

    def start_all(p_ref, s):
        def body(j, carry):
            copy(p_ref, s, j, 0).start(priority=0)
            copy(p_ref, s, j, 1).start(priority=1)
            return carry
        lax.fori_loop(0, tc, body, 0, unroll=8)

    def wait_all(p_ref, s):
        def body(j, carry):
            copy(p_ref, s, j, 0).wait()
            copy(p_ref, s, j, 1).wait()
            return carry
        lax.fori_loop(0, tc, body, 0, unroll=8)

    @pl.when(i == 0)
    def _():
        start_all(pos_ref, 0)

    @pl.when(i + 1 < pl.num_programs(0))
    def _():
        start_all(pos_next_ref, 1 - slot)

    wait_all(pos_ref, slot)
    gate = gate_ref[...]
    f = gate[:, 0:1] * ybuf[slot, 0] + gate[:, 1:2] * ybuf[slot, 1]
    z = _layernorm_rows(ALPHA * x_ref[...] + f, g_ref[...], b_ref[...])
    of_ref[...] = z
    ob_ref[...] = z.astype(BF16)


def _combine_ln(ys, pos, gates, x, g, b, tc):
    T = x.shape[0]
    n = T // tc
    row = pl.BlockSpec((tc, D_MODEL), lambda i: (i, 0))
    vec = pl.BlockSpec((1, D_MODEL), lambda i: (0, 0))
    pos3 = pos.reshape(n, 1, 2 * tc)
    return pl.pallas_call(
        functools.partial(_combine_kernel, tc=tc),
        grid=(n,),
        in_specs=[pl.BlockSpec((None, 1, 2 * tc), lambda i: (i, 0, 0), memory_space=pltpu.SMEM),
                  pl.BlockSpec((None, 1, 2 * tc), lambda i: (jnp.minimum(i + 1, n - 1), 0, 0),
                               memory_space=pltpu.SMEM),
                  pl.BlockSpec((tc, 2), lambda i: (i, 0)),
                  row, vec, vec,
                  pl.BlockSpec(memory_space=pl.ANY)],
        out_specs=[row, row],
        out_shape=[jax.ShapeDtypeStruct((T, D_MODEL), F32), jax.ShapeDtypeStruct((T, D_MODEL), BF16)],
        scratch_shapes=[pltpu.VMEM((2, 2, tc, D_MODEL), F32), pltpu.SemaphoreType.DMA((2,))],
        compiler_params=_cp("arbitrary"),
        name="expert_combine_ln",
    )(pos3, pos3, gates, x, g.reshape(1, -1), b.reshape(1, -1), ys)


def _tiles(B, S):
    T = B * S
    return dict(
        tm_proj=min(512, T), tm_kv=min(512, B * N_MEM_TOKENS),
        ts_conv=min(256, S), att_chunk=min(4096, S), ts_mem=min(1024, S),
        tm_out=min(512, T), tm_ffn=min(512, T), tf=512, tf_moe=1024,
        tm_route=min(512, T), td=min(256, T), tm_moe=min(512, T), tc=min(256, T),
    )


class _Group:
    def __init__(self, x, mem):
        self.B, self.S, _ = x.shape
        self.T = self.B * self.S
        if self.S % (DILATIONS[-1] * ATT_TQ) or self.S % ATT_HALO:
            raise ValueError("sequence length must be a multiple of 2048")
        self.t = _tiles(self.B, self.S)
        self.rope_tabs = _rope_tables(self.S)
        self.x = x.reshape(self.T, D_MODEL)
        self.xb = self.x
        self.mem_b = mem.reshape(self.B * N_MEM_TOKENS, D_MODEL).astype(BF16)


def _mixer(g, l, p, wb):
    B, S, T, t = g.B, g.S, g.T, g.t
    ag, qk, vm = _in_proj(g.xb, wb["w_in"][l], g.rope_tabs, t["tm_proj"])
    kv = _matmul(g.mem_b, wb["w_mem_kv"][l], t["tm_kv"], 2 * D_MEM, BF16)
    kv = kv.reshape(B, N_MEM_TOKENS, 2 * D_MEM)
    conv_o = _conformer_conv(ag.reshape(B, S, 2 * D_CONV), p["conv_w"][l], p["conv_b"][l],
                             p["conv_ln_g"][l], p["conv_ln_b"][l], t["ts_conv"])
    att_o = _dilated_attention(qk, vm, B, S, t["att_chunk"])
    mem_o = _memory_attention(vm, kv, B, S, t["ts_mem"])
    g.x, g.xb = _out_proj_ln(conv_o.reshape(T, D_CONV), att_o, mem_o, g.x, wb["w_out"][l],
                             p["ln1_g"][l], p["ln1_b"][l], t["tm_out"])


def _moe(g, l, p, moe_w):
    t = g.t
    idx, gates = _router(g.x, p["moe_router"][l // 2], t["tm_route"])
    pos, table = _routing_plan(idx, t["tm_moe"])
    xs = _dispatch(g.x, pos, t["td"])
    ys = _moe_ffn(xs, table, *moe_w, t["tm_moe"], t["tf_moe"])
    g.x, g.xb = _combine_ln(ys, pos, gates, g.x, p["ln2_g"][l], p["ln2_b"][l], t["tc"])


def kernel(x_prompt, x_sample, mem_prompt, mem_sample, w_in, conv_w, conv_b, conv_ln_g, conv_ln_b,
           w_mem_kv, w_out, ln1_g, ln1_b, ffn_w1, ffn_w3, ffn_w2, moe_router, moe_w1, moe_w3, moe_w2,
           ln2_g, ln2_b):
    p = dict(conv_w=conv_w, conv_b=conv_b, conv_ln_g=conv_ln_g, conv_ln_b=conv_ln_b,
             ln1_g=ln1_g, ln1_b=ln1_b, ln2_g=ln2_g, ln2_b=ln2_b, moe_router=moe_router)
    wb = dict(w_in=w_in, w_mem_kv=w_mem_kv, w_out=w_out, ffn_w1=ffn_w1, ffn_w3=ffn_w3, ffn_w2=ffn_w2)
    wb = {k: v.astype(BF16) for k, v in wb.items()}
    moe_f32 = (moe_w1, moe_w3, moe_w2)
    groups = [_Group(x_prompt, mem_prompt), _Group(x_sample, mem_sample)]
    host = max(groups, key=lambda g: g.T)
    moe_b = {}
    for l in range(DEPTH):
        i = l // 2
        for g in groups:
            _mixer(g, l, p, wb)
            if l % 2 == 0:
                nxt = (l + 1) // 2
                srcs = ()
                if g is host and l + 1 < DEPTH:
                    srcs = [w[nxt].reshape(-1, w.shape[-1]) for w in moe_f32]
                g.x, g.xb, casts = _ffn_ln(g.xb, g.x, wb["ffn_w1"][i], wb["ffn_w3"][i], wb["ffn_w2"][i],
                                           p["ln2_g"][l], p["ln2_b"][l], g.t["tm_ffn"], g.t["tf"], srcs)
                if casts:
                    moe_b[nxt] = [c.reshape(w.shape[1:]) for c, w in zip(casts, moe_f32)]
            else:
                if i not in moe_b:
                    moe_b[i] = [w[i].astype(BF16) for w in moe_f32]
                _moe(g, l, p, moe_b[i])
    return tuple(g.x.reshape(g.B, g.S, D_MODEL) for g in groups)
```
